```python
import math, functools
import jax, jax.numpy as jnp
from jax import lax
import numpy as np

D_MODEL = 1024
BATCH = 16
SEQ = 2048
DEPTH = 1
DEC_BATCH = 128
DEC_SEQ = 1
PAST_LEN = 8192
PAGE_SIZE = 128

SSM_EXPAND = 2
D_INNER = SSM_EXPAND * D_MODEL
SSM_HEADDIM = 64
SSM_HEADS = D_INNER // SSM_HEADDIM
SSM_GROUPS = 8
SSM_HPG = SSM_HEADS // SSM_GROUPS
D_STATE = 128
CONV_W = 4
CONV_DIM = D_INNER + 2 * SSM_GROUPS * D_STATE
SSD_CHUNK = 128
N_HEADS = 16
N_KV = 4
GQA = N_HEADS // N_KV
HEAD_DIM = 64
CMP_STRIDE = 16
CMP_LEN = 2 * CMP_STRIDE
SLC_BLOCK = 64
SUB_PER_SLC = SLC_BLOCK // CMP_STRIDE
N_SEL = 16
WINDOW = 512
Q_BLOCK = 128
ROPE_THETA = 10000.0
ATTN_SCALE = HEAD_DIM ** -0.5
PEER_HEADS = 8
N_KEYS = 128
N_EXPERTS = N_KEYS * N_KEYS
PEER_TOPK = 16
PEER_DK = 128
PEER_BLOCK = 256
IN_SIZES = (D_INNER, CONV_DIM, SSM_HEADS, N_HEADS * HEAD_DIM, 3 * 2 * N_KV * HEAD_DIM, 3 * N_HEADS, 2 * D_MODEL)
D_IN = sum(IN_SIZES)
EPS = 1e-6
NEG = -1e9
FORCE_BONUS = 1e3
F32 = jnp.float32

kernel_name = "hybrid_ssd_nsa_peer_step"


def rmsnorm(x, w):
    xf = x.astype(F32)
    y = xf * lax.rsqrt(jnp.mean(xf * xf, axis=-1, keepdims=True) + EPS)
    return (y * w.astype(F32)).astype(x.dtype)


def rope(x, pos):
    half = HEAD_DIM // 2
    inv = ROPE_THETA ** (-jnp.arange(half, dtype=F32) / half)
    ang = pos.astype(F32)[:, None] * inv[None, :]
    shape = (1, ang.shape[0]) + (1,) * (x.ndim - 3) + (half,)
    cos = jnp.cos(ang).reshape(shape)
    sin = jnp.sin(ang).reshape(shape)
    x1 = x[..., :half].astype(F32)
    x2 = x[..., half:].astype(F32)
    return jnp.concatenate([x1 * cos - x2 * sin, x2 * cos + x1 * sin], axis=-1).astype(x.dtype)


def masked_softmax(s, mask):
    p = jax.nn.softmax(jnp.where(mask, s.astype(F32), NEG), axis=-1)
    return jnp.where(mask, p, 0.0)


def pad_time(a, n):
    return jnp.pad(a, [(0, 0), (0, n)] + [(0, 0)] * (a.ndim - 2)) if n else a


def in_projection(x, norm_w, w_in):
    h = rmsnorm(x, norm_w)
    u = jnp.einsum("btd,de->bte", h, w_in)
    cuts = np.cumsum(IN_SIZES)[:-1].tolist()
    return jnp.split(u, cuts, axis=-1)


def causal_conv(xbc, prefix, conv_w, conv_b):
    full = jnp.concatenate([prefix.astype(xbc.dtype), xbc], axis=1)
    out = lax.conv_general_dilated(full, conv_w.astype(xbc.dtype)[:, None, :], window_strides=(1,),
                                   padding="VALID", dimension_numbers=("NWC", "WIO", "NWC"),
                                   feature_group_count=CONV_DIM)
    return jax.nn.silu(out + conv_b.astype(xbc.dtype)), full[:, -(CONV_W - 1):]


def ssd(x, dt, a, bm, cm, h0):
    b, t = x.shape[:2]
    q = min(SSD_CHUNK, t)
    nc = -(-t // q)
    pad = nc * q - t
    x, dt, bm, cm = (pad_time(v, pad) for v in (x, dt, bm, cm))
    xs = x.reshape(b, nc, q, SSM_GROUPS, SSM_HPG, SSM_HEADDIM)
    dt = dt.reshape(b, nc, q, SSM_GROUPS, SSM_HPG)
    bm = bm.reshape(b, nc, q, SSM_GROUPS, D_STATE)
    cm = cm.reshape(b, nc, q, SSM_GROUPS, D_STATE)
    cs = jnp.cumsum(dt * a.reshape(SSM_GROUPS, SSM_HPG), axis=2)
    xdt = xs.astype(F32) * dt[..., None]
    causal = jnp.tril(jnp.ones((q, q), bool))[:, :, None, None]
    seg = jnp.where(causal, cs[:, :, :, None] - cs[:, :, None, :], -jnp.inf)
    cb = jnp.einsum("bclgn,bcsgn->bclsg", cm, bm).astype(F32)
    w_ls = cb[..., None] * jnp.exp(seg)
    y_diag = jnp.einsum("bclsgr,bcsgrp->bclgrp", w_ls, xdt)
    chunk_states = jnp.einsum("bcsgn,bcsgrp->bcgrpn", bm.astype(F32),
                              xdt * jnp.exp(cs[:, :, -1:] - cs)[..., None])
    chunk_decay = jnp.exp(cs[:, :, -1])

    def step(h, inp):
        dec, st = inp
        return dec[..., None, None] * h + st, h

    h_init = h0.astype(F32).reshape(b, SSM_GROUPS, SSM_HPG, SSM_HEADDIM, D_STATE)
    h_last, h_in = lax.scan(step, h_init, (jnp.moveaxis(chunk_decay, 1, 0), jnp.moveaxis(chunk_states, 1, 0)))
    h_in = jnp.moveaxis(h_in, 0, 1)
    y_off = jnp.einsum("bclgn,bcgrpn->bclgrp", cm.astype(F32), h_in) * jnp.exp(cs)[..., None]
    y = (y_diag + y_off).reshape(b, nc * q, SSM_HEADS, SSM_HEADDIM)[:, :t]
    return y.astype(x.dtype), h_last.reshape(b, SSM_HEADS, SSM_HEADDIM, D_STATE)


def mamba_branch(z, xbc, dt_raw, conv_prefix, h0, conv_w, conv_b, dt_bias, a_log, d_skip, ssm_norm_w):
    b, t = z.shape[:2]
    xbc_act, conv_state = causal_conv(xbc, conv_prefix, conv_w, conv_b)
    gn = SSM_GROUPS * D_STATE
    xs = xbc_act[..., :D_INNER].reshape(b, t, SSM_HEADS, SSM_HEADDIM)
    bm = xbc_act[..., D_INNER:D_INNER + gn].reshape(b, t, SSM_GROUPS, D_STATE)
    cm = xbc_act[..., D_INNER + gn:].reshape(b, t, SSM_GROUPS, D_STATE)
    dt = jax.nn.softplus(dt_raw.astype(F32) + dt_bias.astype(F32))
    a = -jnp.exp(a_log.astype(F32))
    y, h_last = ssd(xs, dt, a, bm, cm, h0)
    y = (y + xs * d_skip[:, None]).reshape(b, t, D_INNER) * jax.nn.silu(z)
    y = rmsnorm(y.reshape(b, t, SSM_GROUPS, D_INNER // SSM_GROUPS), ssm_norm_w.reshape(SSM_GROUPS, -1))
    return y.reshape(b, t, D_INNER), conv_state, h_last


def nsa_rows(q, kv, pos, q_norm_w, k_norm_w):
    b, t = q.shape[:2]
    qn = rmsnorm(q.reshape(b, t, N_KV, GQA, HEAD_DIM), q_norm_w)
    qr = rope(qn, pos)
    kv = kv.reshape(b, t, 3, 2, N_KV, HEAD_DIM)

    def rotated(i):
        k = rope(rmsnorm(kv[:, :, i, 0], k_norm_w[i]), pos)
        return jnp.stack([k, kv[:, :, i, 1]], axis=2)

    return qn, qr, kv[:, :, 0], rotated(1), rotated(2)


def sub_block_proj(rows, w_half):
    b, l = rows.shape[:2]
    sub = rows.reshape(b, l // CMP_STRIDE, CMP_STRIDE, 2, N_KV, HEAD_DIM)
    return jnp.einsum("bnlchd,lcde->bnche", sub, w_half)


def compress(parts, cmp_pos, w_cmp, k_gain):
    first = jnp.concatenate([sub_block_proj(r, w_cmp[:CMP_STRIDE]) for r in parts], axis=1)
    second = jnp.concatenate([sub_block_proj(r, w_cmp[CMP_STRIDE:]) for r in parts], axis=1)
    bias = jnp.einsum("lcd,lcde->ce", cmp_pos, w_cmp)
    kvc = first[:, :-1] + second[:, 1:] + bias[None, None, :, None, :]
    return rmsnorm(kvc[:, :, 0], k_gain), kvc[:, :, 1]


def gather_rows(kv, tok):
    bidx = jnp.arange(kv.shape[0])[:, None, None, None]
    hidx = jnp.arange(N_KV)[None, None, :, None]
    g = kv[bidx, tok, :, hidx]
    return g[..., 0, :], g[..., 1, :]


def gather_paged(pool, layer, page_table, new_rows, tok):
    past = page_table.shape[1] * PAGE_SIZE
    bidx = jnp.arange(page_table.shape[0])[:, None, None, None]
    hidx = jnp.arange(N_KV)[None, None, :, None]
    tp = jnp.minimum(tok, past - 1)
    phys = page_table[bidx, tp // PAGE_SIZE]
    g_past = pool[layer, phys, tp % PAGE_SIZE, :, hidx]
    tn = jnp.clip(tok - past, 0, new_rows.shape[1] - 1)
    g_new = new_rows[bidx, tn, :, hidx]
    g = jnp.where((tok < past)[..., None, None], g_past, g_new)
    return g[..., 0, :], g[..., 1, :]


def nsa_cmp_slc(qn, qr, q_pos, kc, vc, fetch, n_keys):
    nc = kc.shape[1]
    c_end = jnp.arange(nc) * CMP_STRIDE + CMP_LEN - 1
    c_mask = (c_end[None, :] <= q_pos[:, None])[None, :, None, None, :]
    s = jnp.einsum("btkgd,bnkd->btkgn", qn, kc) * ATTN_SCALE
    p = masked_softmax(s, c_mask)
    o_cmp = jnp.einsum("btkgn,bnkd->btkgd", p.astype(vc.dtype), vc)
    ns = -(-n_keys // SLC_BLOCK)
    imp = p.sum(axis=3)
    imp = jnp.pad(imp, ((0, 0), (0, 0), (0, 0), (0, SUB_PER_SLC * ns - nc)))
    main = imp.reshape(imp.shape[:3] + (ns, SUB_PER_SLC)).sum(-1)
    prev = jnp.pad(imp[..., SUB_PER_SLC - 1::SUB_PER_SLC][..., :ns - 1], ((0, 0), (0, 0), (0, 0), (1, 0)))
    score = main + prev
    j = jnp.arange(ns)[None, :]
    cur = (q_pos // SLC_BLOCK)[:, None]
    forced = (j == 0) | (j == cur) | (j == cur - 1)
    valid = j * SLC_BLOCK <= q_pos[:, None]
    score = jnp.where(forced[None, :, None, :], score + FORCE_BONUS, score)
    score = jnp.where(valid[None, :, None, :], score, NEG)
    top_s, top_j = lax.top_k(score, min(N_SEL, ns))
    tok = top_j[..., None] * SLC_BLOCK + jnp.arange(SLC_BLOCK)
    ok = (top_s > NEG / 2)[..., None] & (tok <= q_pos[None, :, None, None, None])
    tok = tok.reshape(tok.shape[:3] + (-1,))
    ok = ok.reshape(ok.shape[:3] + (-1,))
    ks, vs = fetch(jnp.minimum(tok, n_keys - 1))
    s2 = jnp.einsum("btkgd,btkjd->btkgj", qr, ks) * ATTN_SCALE
    p2 = masked_softmax(s2, ok[:, :, :, None, :])
    o_slc = jnp.einsum("btkgj,btkjd->btkgd", p2.astype(vs.dtype), vs)
    return o_cmp, o_slc


def window_attn(qr, q_pos, kv, k_pos):
    s = jnp.einsum("btkgd,bjkd->btkgj", qr, kv[:, :, 0]) * ATTN_SCALE
    kp = k_pos[None, :]
    qp = q_pos[:, None]
    mask = (kp <= qp) & (kp > qp - WINDOW) & (kp >= 0)
    p = masked_softmax(s, mask[None, :, None, None, :])
    return jnp.einsum("btkgj,bjkd->btkgd", p.astype(kv.dtype), kv[:, :, 1])


def nsa_prompt(qn, qr, cmp_rows, slc_rows, win_rows, cmp_pos, w_cmp, k_gain, win_buf):
    b, t = qn.shape[:2]
    nqb = t // Q_BLOCK
    kc, vc = compress([cmp_rows], cmp_pos, w_cmp, k_gain)
    blk = (b * nqb, 1, Q_BLOCK, N_KV, GQA, HEAD_DIM)
    full = (b, t, N_KV, GQA, HEAD_DIM)

    def cmp_slc_item(args):
        bi, ni, q_blk, qr_blk = args
        q_pos = ni * Q_BLOCK + jnp.arange(Q_BLOCK, dtype=jnp.int32)
        kv_b = lax.dynamic_index_in_dim(slc_rows, bi, 0, keepdims=True)
        return nsa_cmp_slc(q_blk, qr_blk, q_pos,
                           lax.dynamic_index_in_dim(kc, bi, 0, keepdims=True),
                           lax.dynamic_index_in_dim(vc, bi, 0, keepdims=True),
                           functools.partial(gather_rows, kv_b), t)

    item = jnp.arange(b * nqb, dtype=jnp.int32)
    o_cmp, o_slc = lax.map(cmp_slc_item, (item // nqb, item % nqb, qn.reshape(blk), qr.reshape(blk)))
    o_cmp = o_cmp.reshape(full)
    o_slc = o_slc.reshape(full)
    kv_pad = jnp.pad(win_rows, ((0, 0), (WINDOW, 0), (0, 0), (0, 0), (0, 0)))

    def win_item(args):
        ni, qr_blk = args
        start = ni * Q_BLOCK
        kv_blk = lax.dynamic_slice_in_dim(kv_pad, start, Q_BLOCK + WINDOW, axis=1)
        k_pos = start - WINDOW + jnp.arange(Q_BLOCK + WINDOW, dtype=jnp.int32)
        q_pos = start + jnp.arange(Q_BLOCK, dtype=jnp.int32)
        return window_attn(qr_blk, q_pos, kv_blk, k_pos)

    qr_blocks = jnp.moveaxis(qr.reshape(b, nqb, Q_BLOCK, N_KV, GQA, HEAD_DIM), 1, 0)
    o_win = lax.map(win_item, (jnp.arange(nqb, dtype=jnp.int32), qr_blocks))
    o_win = jnp.moveaxis(o_win, 0, 1).reshape(full)
    new_win = jnp.pad(win_rows, ((0, 0), (win_buf, 0), (0, 0), (0, 0), (0, 0)))[:, -win_buf:]
    return o_cmp, o_slc, o_win, new_win


def nsa_sample(qn, qr, cmp_rows, slc_rows, win_rows, cache_cmp_kv, cache_slc_kv, layer, page_table,
               win_state, cmp_pos, w_cmp, k_gain):
    b, t = qn.shape[:2]
    past = page_table.shape[1] * PAGE_SIZE
    past_cmp = cache_cmp_kv[layer, page_table].reshape(b, past, 2, N_KV, HEAD_DIM)
    n_new = (t // CMP_STRIDE) * CMP_STRIDE
    parts = [past_cmp] + ([cmp_rows[:, :n_new]] if n_new else [])
    kc, vc = compress(parts, cmp_pos, w_cmp, k_gain)
    q_pos = past + jnp.arange(t, dtype=jnp.int32)
    o_cmp, o_slc = nsa_cmp_slc(qn, qr, q_pos, kc, vc,
                               functools.partial(gather_paged, cache_slc_kv, layer, page_table, slc_rows),
                               past + t)
    win_buf = win_state.shape[1]
    kv_w = jnp.concatenate([win_state.astype(win_rows.dtype), win_rows], axis=1)
    k_pos = past - win_buf + jnp.arange(win_buf + t, dtype=jnp.int32)
    o_win = window_attn(qr, q_pos, kv_w, k_pos)
    return o_cmp, o_slc, o_win, kv_w[:, -win_buf:]


def peer(h, w_pq, sub_keys, expert_u, expert_v):
    n = h.shape[0]
    blk = min(PEER_BLOCK, n)
    nb = -(-n // blk)
    hp = jnp.pad(h, ((0, nb * blk - n), (0, 0))).reshape(nb, blk, D_MODEL)

    def one(hb):
        q = (hb @ w_pq).reshape(blk, PEER_HEADS, 2, PEER_DK // 2)
        s = jnp.einsum("nhcd,hckd->nhck", q, sub_keys).astype(F32)
        s1, i1 = lax.top_k(s[:, :, 0], PEER_TOPK)
        s2, i2 = lax.top_k(s[:, :, 1], PEER_TOPK)
        cand = (s1[..., :, None] + s2[..., None, :]).reshape(blk, PEER_HEADS, -1)
        cidx = (i1[..., :, None] * N_KEYS + i2[..., None, :]).reshape(blk, PEER_HEADS, -1)
        top_s, top_c = lax.top_k(cand, PEER_TOPK)
        e = jnp.take_along_axis(cidx, top_c, axis=-1)
        g = jax.nn.softmax(top_s, axis=-1)
        u = expert_u[e]
        v = expert_v[e]
        act = jax.nn.gelu(jnp.einsum("nd,nhkd->nhk", hb, u).astype(F32), approximate=False)
        return jnp.einsum("nhk,nhkd->nd", (g * act).astype(v.dtype), v)

    return lax.map(one, hp).reshape(nb * blk, D_MODEL)[:n]


def layer(x, pos, nsa_attend, conv_prefix, h0, norm1_w, w_in, conv_w, conv_b, dt_bias, a_log, d_skip,
          ssm_norm_w, w_proj_ssm, q_norm_w, k_norm_w, w_proj_attn, w_out, norm2_w, w_pq, sub_keys,
          expert_u, expert_v):
    b, t = x.shape[:2]
    z, xbc, dt_raw, q, kv, nsa_gate, merge_gate = in_projection(x, norm1_w, w_in)
    y_ssm, conv_state, h_last = mamba_branch(z, xbc, dt_raw, conv_prefix, h0, conv_w, conv_b, dt_bias,
                                             a_log, d_skip, ssm_norm_w)
    qn, qr, cmp_rows, slc_rows, win_rows = nsa_rows(q, kv, pos, q_norm_w, k_norm_w)
    o_cmp, o_slc, o_win, win_state = nsa_attend(qn, qr, cmp_rows, slc_rows, win_rows)
    g = jax.nn.sigmoid(nsa_gate.astype(F32)).reshape(b, t, 3, N_KV, GQA, 1)
    o = g[:, :, 0] * o_cmp + g[:, :, 1] * o_slc + g[:, :, 2] * o_win
    o = o.astype(x.dtype).reshape(b, t, N_HEADS * HEAD_DIM)
    gate_a, gate_b = jnp.split(jax.nn.sigmoid(merge_gate), 2, axis=-1)
    mixed = gate_a * (y_ssm @ w_proj_ssm) + gate_b * (o @ w_proj_attn)
    x = x + mixed @ w_out
    h2 = rmsnorm(x, norm2_w)
    x = x + peer(h2.reshape(b * t, D_MODEL), w_pq, sub_keys, expert_u, expert_v).reshape(b, t, D_MODEL)
    return x, (cmp_rows, slc_rows, win_state, h_last, conv_state)


def stack_layers(states, i):
    return jnp.stack([s[i] for s in states], axis=0)


def setup_inputs(seed: int = 0) -> dict:
    key = jax.random.key(seed)
    ks = iter(jax.random.split(key, 40))

    def nrm(shape, scale):
        return jax.random.normal(next(ks), shape, F32) * scale

    n_pages = PAST_LEN // PAGE_SIZE
    n_used = DEC_BATCH * n_pages
    n_pool = n_used + max(1, n_used // 4)
    win_buf = min(WINDOW, PAST_LEN)
    page_table = jax.random.permutation(next(ks), n_pool)[:n_used].reshape(DEC_BATCH, n_pages).astype(jnp.int32)
    dt0 = jnp.exp(jax.random.uniform(next(ks), (DEPTH, SSM_HEADS), F32, math.log(1e-3), math.log(1e-1)))
    return {
        "x_prompt": nrm((BATCH, SEQ, D_MODEL), 1.0),
        "x_sample": nrm((DEC_BATCH, DEC_SEQ, D_MODEL), 1.0),
        "cache_cmp_kv": nrm((DEPTH, n_pool, PAGE_SIZE, 2, N_KV, HEAD_DIM), 1.0),
        "cache_slc_kv": nrm((DEPTH, n_pool, PAGE_SIZE, 2, N_KV, HEAD_DIM), 1.0),
        "page_table": page_table,
        "state_win_kv": nrm((DEPTH, DEC_BATCH, win_buf, 2, N_KV, HEAD_DIM), 1.0),
        "state_ssm": nrm((DEPTH, DEC_BATCH, SSM_HEADS, SSM_HEADDIM, D_STATE), 0.1),
        "state_conv": nrm((DEPTH, DEC_BATCH, CONV_W - 1, CONV_DIM), 1.0),
        "norm1_w": 1.0 + nrm((DEPTH, D_MODEL), 0.02),
        "w_in": nrm((DEPTH, D_MODEL, D_IN), D_MODEL ** -0.5),
        "conv_w": nrm((DEPTH, CONV_W, CONV_DIM), CONV_W ** -0.5),
        "conv_b": nrm((DEPTH, CONV_DIM), 0.02),
        "dt_bias": dt0 + jnp.log(-jnp.expm1(-dt0)),
        "a_log": jnp.log(jax.random.uniform(next(ks), (DEPTH, SSM_HEADS), F32, 1.0, 16.0)),
        "d_skip": 1.0 + nrm((DEPTH, SSM_HEADS), 0.1),
        "ssm_norm_w": 1.0 + nrm((DEPTH, D_INNER), 0.02),
        "w_proj_ssm": nrm((DEPTH, D_INNER, D_MODEL), D_INNER ** -0.5),
        "q_norm_w": 1.0 + nrm((DEPTH, HEAD_DIM), 0.02),
        "k_norm_w": 1.0 + nrm((DEPTH, 3, HEAD_DIM), 0.02),
        "cmp_pos": nrm((DEPTH, CMP_LEN, 2, HEAD_DIM), 0.1),
        "w_cmp": nrm((DEPTH, CMP_LEN, 2, HEAD_DIM, HEAD_DIM), (CMP_LEN * HEAD_DIM) ** -0.5),
        "w_proj_attn": nrm((DEPTH, N_HEADS * HEAD_DIM, D_MODEL), (N_HEADS * HEAD_DIM) ** -0.5),
        "w_out": nrm((DEPTH, D_MODEL, D_MODEL), D_MODEL ** -0.5),
        "norm2_w": 1.0 + nrm((DEPTH, D_MODEL), 0.02),
        "w_pq": nrm((DEPTH, D_MODEL, PEER_HEADS * PEER_DK), D_MODEL ** -0.5),
        "sub_keys": nrm((DEPTH, PEER_HEADS, 2, N_KEYS, PEER_DK // 2), (PEER_DK // 2) ** -0.5),
        "expert_u": nrm((DEPTH, N_EXPERTS, D_MODEL), D_MODEL ** -0.5),
        "expert_v": nrm((DEPTH, N_EXPERTS, D_MODEL), PEER_HEADS ** -0.5),
    }


def reference(x_prompt, x_sample, cache_cmp_kv, cache_slc_kv, page_table, state_win_kv, state_ssm, state_conv,
              norm1_w, w_in, conv_w, conv_b, dt_bias, a_log, d_skip, ssm_norm_w, w_proj_ssm, q_norm_w, k_norm_w,
              cmp_pos, w_cmp, w_proj_attn, w_out, norm2_w, w_pq, sub_keys, expert_u, expert_v):
    bp, tp = x_prompt.shape[:2]
    ts = x_sample.shape[1]
    past = page_table.shape[1] * PAGE_SIZE
    win_buf = state_win_kv.shape[2]
    pos_p = jnp.arange(tp, dtype=jnp.int32)
    pos_s = past + jnp.arange(ts, dtype=jnp.int32)
    yp, ys = x_prompt, x_sample
    new_p, new_s = [], []
    for l in range(DEPTH):
        shared = (norm1_w[l], w_in[l], conv_w[l], conv_b[l], dt_bias[l], a_log[l], d_skip[l], ssm_norm_w[l],
                  w_proj_ssm[l], q_norm_w[l], k_norm_w[l], w_proj_attn[l], w_out[l], norm2_w[l], w_pq[l],
                  sub_keys[l], expert_u[l], expert_v[l])
        attend_p = functools.partial(nsa_prompt, cmp_pos=cmp_pos[l], w_cmp=w_cmp[l], k_gain=k_norm_w[l, 0],
                                     win_buf=win_buf)
        attend_s = functools.partial(nsa_sample, cache_cmp_kv=cache_cmp_kv, cache_slc_kv=cache_slc_kv, layer=l,
                                     page_table=page_table, win_state=state_win_kv[l], cmp_pos=cmp_pos[l],
                                     w_cmp=w_cmp[l], k_gain=k_norm_w[l, 0])
        conv0 = jnp.zeros((bp, CONV_W - 1, CONV_DIM), x_prompt.dtype)
        h0 = jnp.zeros((bp, SSM_HEADS, SSM_HEADDIM, D_STATE), F32)
        yp, st_p = layer(yp, pos_p, attend_p, conv0, h0, *shared)
        ys, st_s = layer(ys, pos_s, attend_s, state_conv[l], state_ssm[l], *shared)
        new_p.append(st_p)
        new_s.append(st_s)
    return (yp, ys, stack_layers(new_p, 0), stack_layers(new_s, 0), stack_layers(new_p, 1), stack_layers(new_s, 1),
            stack_layers(new_p, 2), stack_layers(new_s, 2), stack_layers(new_p, 3), stack_layers(new_s, 3),
            stack_layers(new_p, 4), stack_layers(new_s, 4))
```

```python
import functools
import math

import jax
import jax.numpy as jnp
import numpy as np
from jax import lax
from jax.experimental import pallas as pl
from jax.experimental.pallas import tpu as pltpu

F32 = jnp.float32
BF16 = jnp.bfloat16
HI = lax.Precision.HIGHEST

D_MODEL = 1024
PAGE_SIZE = 128
D_INNER = 2048
SSM_HEADDIM = 64
SSM_HEADS = 32
SSM_GROUPS = 8
SSM_HPG = 4
D_STATE = 128
CONV_W = 4
CONV_DIM = 4096
SSD_CHUNK = 128
N_HEADS = 16
N_KV = 4
GQA = 4
HEAD_DIM = 64
CMP_STRIDE = 16
CMP_LEN = 32
SLC_BLOCK = 64
SUB_PER_SLC = 4
N_SEL = 16
WINDOW = 512
Q_BLOCK = 128
ROPE_THETA = 10000.0
ATTN_SCALE = HEAD_DIM ** -0.5
PEER_HEADS = 8
N_KEYS = 128
PEER_TOPK = 16
PEER_DK = 128
EPS = 1e-6
NEG = -1e9
FORCE_BONUS = 1e3

LANES = 128
KV_W = N_KV * HEAD_DIM
ROW_W = 2 * KV_W
ROW_TILES = ROW_W // LANES
C_XBC, C_Z, C_MERGE, C_Q, C_KV, C_SMALL = 0, 4096, 6144, 8192, 9216, 10752
D_IN_P = 10880
IN_TILE_N = 2176
VMEM_LIMIT = 56 * 1024 * 1024


def _cparams(sem, vmem=VMEM_LIMIT):
    return pltpu.CompilerParams(dimension_semantics=sem, vmem_limit_bytes=vmem)


def _nt(a, b, precision=None):
    return lax.dot_general(a, b, (((1,), (1,)), ((), ())), preferred_element_type=F32, precision=precision)


def _tn(a, b, precision=None):
    return lax.dot_general(a, b, (((0,), (0,)), ((), ())), preferred_element_type=F32, precision=precision)


def _mm(a, b, precision=None):
    return jnp.dot(a, b, preferred_element_type=F32, precision=precision)


def _sigmoid(x):
    return 1.0 / (1.0 + jnp.exp(-x))


def _silu(x):
    return x * _sigmoid(x)


def _softplus(x):
    return jnp.maximum(x, 0.0) + jnp.log1p(jnp.exp(-jnp.abs(x)))


def _inproj_kernel(x_ref, nw_ref, w_ref, o_ref):
    x = x_ref[...]
    h = x * lax.rsqrt(jnp.mean(x * x, axis=-1, keepdims=True) + EPS) * nw_ref[...]
    o_ref[...] = _mm(h.astype(BF16), w_ref[...])


def _inproj(x2d, norm_w, w_p):
    n = x2d.shape[0]
    tm = min(512, n)
    return pl.pallas_call(
        _inproj_kernel,
        grid=(D_IN_P // IN_TILE_N, n // tm),
        in_specs=[pl.BlockSpec((tm, D_MODEL), lambda j, i: (i, 0)),
                  pl.BlockSpec((1, D_MODEL), lambda j, i: (0, 0)),
                  pl.BlockSpec((D_MODEL, IN_TILE_N), lambda j, i: (0, j))],
        out_specs=pl.BlockSpec((tm, IN_TILE_N), lambda j, i: (i, j)),
        out_shape=jax.ShapeDtypeStruct((n, D_IN_P), F32),
        compiler_params=_cparams(("arbitrary", "arbitrary")),
        name="inproj",
    )(x2d, norm_w.reshape(1, D_MODEL), w_p)


def _ssd_kernel(xbc_ref, z_ref, sm_ref, h0_ref, pre_ref, cw_ref, cb_ref, dtb_ref, alog_ref, dsk_ref, nw_ref,
                y_ref, hl_ref, st_ref, xe_ref, *, q, qp):
    c = pl.program_id(1)

    @pl.when(c == 0)
    def _():
        st_ref[...] = h0_ref[0]
        xe_ref[0:8, :] = pre_ref[0]

    @pl.when(c > 0)
    def _():
        xe_ref[0:8, :] = xe_ref[qp:qp + 8, :]

    xe_ref[8:8 + q, :] = xbc_ref[0]
    if qp > q:
        xe_ref[8 + q:8 + qp, :] = jnp.zeros((qp - q, CONV_DIM), F32)

    conv = cb_ref[...] + xe_ref[8:8 + qp, :] * cw_ref[3:4, :]
    for s in range(1, CONV_W):
        conv = conv + xe_ref[8 - s:8 - s + qp, :] * cw_ref[3 - s:4 - s, :]
    act = _silu(conv)

    lane = lax.broadcasted_iota(jnp.int32, (1, LANES), 1)
    head_lane = lane < SSM_HEADS
    sm = sm_ref[0]
    if qp > q:
        sm = jnp.concatenate([sm, jnp.zeros((qp - q, LANES), F32)], axis=0)
    dt = _softplus(sm + dtb_ref[...])
    dt = jnp.where(head_lane, dt, 0.0)
    if qp > q:
        row = lax.broadcasted_iota(jnp.int32, (qp, 1), 0)
        dt = jnp.where(row < q, dt, 0.0)
    a = jnp.where(head_lane, -jnp.exp(alog_ref[...]), 0.0)
    da = dt * a
    ri = lax.broadcasted_iota(jnp.int32, (qp, qp), 0)
    ci = lax.broadcasted_iota(jnp.int32, (qp, qp), 1)
    causal = ci <= ri
    cs = _mm(causal.astype(F32), da, precision=HI)
    eye = (lax.broadcasted_iota(jnp.int32, (LANES, LANES), 0)
           == lax.broadcasted_iota(jnp.int32, (LANES, LANES), 1)).astype(F32)
    cs_t = _nt(eye, cs, precision=HI)
    cs_last = cs[qp - 1:qp, :]
    e_cs = jnp.exp(cs)
    e_end = jnp.exp(cs_last - cs)
    e_last = jnp.exp(cs_last)

    for g in range(SSM_GROUPS):
        bm = act[:, D_INNER + g * D_STATE:D_INNER + (g + 1) * D_STATE].astype(BF16)
        cm = act[:, D_INNER + SSM_GROUPS * D_STATE + g * D_STATE:
                 D_INNER + SSM_GROUPS * D_STATE + (g + 1) * D_STATE].astype(BF16)
        cbm = _nt(cm, bm)
        r0 = g * SSM_HPG * SSM_HEADDIM
        s_g = st_ref[r0:r0 + SSM_HPG * SSM_HEADDIM, :]
        y_off = _nt(cm, s_g.astype(BF16))
        ys, xds, decs = [], [], []
        for r in range(SSM_HPG):
            h = g * SSM_HPG + r
            xs = act[:, h * SSM_HEADDIM:(h + 1) * SSM_HEADDIM]
            col = cs[:, h:h + 1]
            rowv = cs_t[h:h + 1, :]
            lmat = jnp.exp(jnp.where(causal, col - rowv, -jnp.inf))
            xdt = xs * dt[:, h:h + 1]
            y_d = _mm((cbm * lmat).astype(BF16), xdt.astype(BF16))
            ys.append(y_d + y_off[:, r * SSM_HEADDIM:(r + 1) * SSM_HEADDIM] * e_cs[:, h:h + 1]
                      + xs * dsk_ref[:, h:h + 1])
            xds.append(xdt * e_end[:, h:h + 1])
            decs.append(jnp.broadcast_to(e_last[:, h:h + 1], (SSM_HEADDIM, 1)))
        xd = jnp.concatenate(xds, axis=1)
        new = _tn(xd.astype(BF16), bm)
        dec = jnp.concatenate(decs, axis=0)
        st_ref[r0:r0 + SSM_HPG * SSM_HEADDIM, :] = dec * s_g + new
        yg = jnp.concatenate(ys, axis=1)
        w = D_INNER // SSM_GROUPS
        zg = z_ref[0][:, g * w:(g + 1) * w]
        if qp > q:
            yg = yg[:q]
        yg = yg * _silu(zg)
        yg = yg * lax.rsqrt(jnp.mean(yg * yg, axis=-1, keepdims=True) + EPS) * nw_ref[:, g * w:(g + 1) * w]
        y_ref[0, :, g * w:(g + 1) * w] = yg

    @pl.when(c == pl.num_programs(1) - 1)
    def _():
        hl_ref[0] = st_ref[...]


def _pad_lanes(v, width=LANES):
    v = v.reshape(1, -1)
    return jnp.pad(v, ((0, 0), (0, width - v.shape[1])))


def _ssd(u3, h0, prefix8, conv_w, conv_b, dt_bias, a_log, d_skip, ssm_norm_w):
    b, t = u3.shape[:2]
    q = min(SSD_CHUNK, t)
    qp = max(q, 8)
    nc = t // q
    row = lambda shape: pl.BlockSpec(shape, lambda i, c: (0, 0))
    return pl.pallas_call(
        functools.partial(_ssd_kernel, q=q, qp=qp),
        grid=(b, nc),
        in_specs=[pl.BlockSpec((1, q, CONV_DIM), lambda i, c: (i, c, C_XBC // CONV_DIM)),
                  pl.BlockSpec((1, q, D_INNER), lambda i, c: (i, c, C_Z // D_INNER)),
                  pl.BlockSpec((1, q, LANES), lambda i, c: (i, c, C_SMALL // LANES)),
                  pl.BlockSpec((1, D_INNER, D_STATE), lambda i, c: (i, 0, 0)),
                  pl.BlockSpec((1, 8, CONV_DIM), lambda i, c: (i, 0, 0)),
                  row((CONV_W, CONV_DIM)), row((1, CONV_DIM)), row((1, LANES)), row((1, LANES)), row((1, LANES)),
                  row((1, D_INNER))],
        out_specs=[pl.BlockSpec((1, q, D_INNER), lambda i, c: (i, c, 0)),
                   pl.BlockSpec((1, D_INNER, D_STATE), lambda i, c: (i, 0, 0))],
        out_shape=[jax.ShapeDtypeStruct((b, t, D_INNER), F32),
                   jax.ShapeDtypeStruct((b, D_INNER, D_STATE), F32)],
        scratch_shapes=[pltpu.VMEM((D_INNER, D_STATE), F32), pltpu.VMEM((qp + 8, CONV_DIM), F32)],
        compiler_params=_cparams(("arbitrary", "arbitrary")),
        name="ssd",
    )(u3, u3, u3, h0, prefix8, conv_w, conv_b.reshape(1, CONV_DIM), _pad_lanes(dt_bias), _pad_lanes(a_log),
      _pad_lanes(d_skip), ssm_norm_w.reshape(1, D_INNER))


def _head_sumsq(x, bd):
    sq = x * x
    hi = sq.astype(BF16)
    lo = (sq - hi.astype(F32)).astype(BF16)
    return _mm(hi, bd) + _mm(lo, bd)


def _rope_tiles(x, cos, sin_signed):
    n_tiles = x.shape[1] // LANES
    lane = lax.broadcasted_iota(jnp.int32, (1, LANES), 1)
    first_half = (lane % HEAD_DIM) < (HEAD_DIM // 2)
    outs = []
    for i in range(n_tiles):
        xt = x[:, i * LANES:(i + 1) * LANES]
        swapped = jnp.where(first_half, pltpu.roll(xt, LANES - HEAD_DIM // 2, 1), pltpu.roll(xt, HEAD_DIM // 2, 1))
        outs.append(xt * cos + swapped * sin_signed)
    return jnp.concatenate(outs, axis=1)


def _nsa_prep_kernel(q_ref, kv_ref, cos_ref, sin_ref, bd_ref, qw_ref, kw_ref, qn_ref, qr_ref, slc_ref, win_ref):
    cos = cos_ref[...]
    sin = sin_ref[...]
    bd = bd_ref[...]
    q = q_ref[...]
    qn = q * lax.rsqrt(_head_sumsq(q, bd) * (1.0 / HEAD_DIM) + EPS) * qw_ref[...]
    qn_ref[...] = qn
    qr_ref[...] = _rope_tiles(qn, cos, sin)
    for br, o_ref in ((1, slc_ref), (2, win_ref)):
        k = kv_ref[:, br * ROW_W:br * ROW_W + KV_W]
        kn = k * lax.rsqrt(_head_sumsq(k, bd[:KV_W, :KV_W]) * (1.0 / HEAD_DIM) + EPS) * kw_ref[br:br + 1, :]
        o_ref[:, :KV_W] = _rope_tiles(kn, cos, sin)
        o_ref[:, KV_W:] = kv_ref[:, br * ROW_W + KV_W:(br + 1) * ROW_W]


def _nsa_prep(u2d, cos_t, sin_t, q_norm_w, k_norm_w, t):
    n = u2d.shape[0]
    tm = min(512, n, t) if t > 1 else n
    tab_blocks = max(t // tm, 1)
    if t == 1:
        cos_t = jnp.broadcast_to(cos_t, (tm, LANES))
        sin_t = jnp.broadcast_to(sin_t, (tm, LANES))
    head_id = np.arange(N_HEADS * HEAD_DIM) // HEAD_DIM
    bd = jnp.asarray(head_id[:, None] == head_id[None, :], BF16)
    qw = jnp.tile(q_norm_w, N_HEADS).reshape(1, -1)
    kw = jnp.tile(k_norm_w, (1, N_KV))
    tab = lambda: pl.BlockSpec((tm, LANES), lambda i: (i % tab_blocks, 0))
    return pl.pallas_call(
        _nsa_prep_kernel,
        grid=(n // tm,),
        in_specs=[pl.BlockSpec((tm, N_HEADS * HEAD_DIM), lambda i: (i, C_Q // (N_HEADS * HEAD_DIM))),
                  pl.BlockSpec((tm, 3 * ROW_W), lambda i: (i, C_KV // (3 * ROW_W))),
                  tab(), tab(),
                  pl.BlockSpec((N_HEADS * HEAD_DIM, N_HEADS * HEAD_DIM), lambda i: (0, 0)),
                  pl.BlockSpec((1, N_HEADS * HEAD_DIM), lambda i: (0, 0)),
                  pl.BlockSpec((3, KV_W), lambda i: (0, 0))],
        out_specs=[pl.BlockSpec((tm, N_HEADS * HEAD_DIM), lambda i: (i, 0)),
                   pl.BlockSpec((tm, N_HEADS * HEAD_DIM), lambda i: (i, 0)),
                   pl.BlockSpec((tm, ROW_W), lambda i: (i, 0)),
                   pl.BlockSpec((tm, ROW_W), lambda i: (i, 0))],
        out_shape=[jax.ShapeDtypeStruct((n, N_HEADS * HEAD_DIM), F32),
                   jax.ShapeDtypeStruct((n, N_HEADS * HEAD_DIM), F32),
                   jax.ShapeDtypeStruct((n, ROW_W), F32),
                   jax.ShapeDtypeStruct((n, ROW_W), F32)],
        compiler_params=_cparams(("arbitrary",)),
        name="nsa_prep",
    )(u2d, u2d, cos_t, sin_t, bd, qw, kw)


def _rope_tables(pos):
    half = HEAD_DIM // 2
    inv = ROPE_THETA ** (-jnp.arange(half, dtype=F32) / half)
    ang = pos.astype(F32)[:, None] * inv[None, :]
    cos = jnp.cos(ang)
    sin = jnp.sin(ang)
    reps = LANES // HEAD_DIM
    return (jnp.tile(jnp.concatenate([cos, cos], axis=1), (1, reps)),
            jnp.tile(jnp.concatenate([-sin, sin], axis=1), (1, reps)))


def _sub_block_proj(load_rows, wbd_ref, pos_ref):
    acc = [[None, None], [None, None]]
    per = KV_W // LANES
    for l in range(CMP_STRIDE):
        for cidx in range(2):
            rows = jnp.concatenate([load_rows(l, cidx * per + j) for j in range(per)], axis=1)
            for half in range(2):
                x = rows + pos_ref[half * CMP_STRIDE + l, cidx:cidx + 1, :]
                part = _mm(x.astype(BF16), wbd_ref[half * CMP_STRIDE + l, cidx])
                acc[half][cidx] = part if acc[half][cidx] is None else acc[half][cidx] + part
    return acc


def _compress_finish(first_k, first_v, second_k, second_v, kg):
    n_sub = first_k.shape[0]
    k = first_k + pltpu.roll(second_k, n_sub - 1, 0)
    v = first_v + pltpu.roll(second_v, n_sub - 1, 0)
    parts = []
    for hd in range(N_KV):
        kh = k[:, hd * HEAD_DIM:(hd + 1) * HEAD_DIM]
        parts.append(kh * lax.rsqrt(jnp.mean(kh * kh, axis=-1, keepdims=True) + EPS))
    return jnp.concatenate(parts, axis=1) * kg, v


def _compress_kernel(rows_ref, wbd_ref, pos_ref, kg_ref, kc_ref, vc_ref, *, n_sub):
    acc = _sub_block_proj(lambda l, j: rows_ref[0, pl.ds(l * ROW_TILES + j, n_sub, stride=CMP_STRIDE * ROW_TILES), :],
                          wbd_ref, pos_ref)
    kc, vc = _compress_finish(acc[0][0], acc[0][1], acc[1][0], acc[1][1], kg_ref[...])
    kc_ref[0] = kc
    vc_ref[0] = vc


def _compress_weights(w_cmp, cmp_pos, k_gain):
    eye = jnp.eye(N_KV, dtype=F32)
    wbd = jnp.einsum("hg,lcde->lchdge", eye, w_cmp).reshape(CMP_LEN, 2, KV_W, KV_W).astype(BF16)
    pos = jnp.tile(cmp_pos, (1, 1, N_KV))
    kg = jnp.tile(k_gain, N_KV).reshape(1, KV_W)
    return wbd, pos, kg


def _compress_prompt(cmp_rows, wbd, pos, kg):
    b, t = cmp_rows.shape[:2]
    cmp_rows = cmp_rows.reshape(b, t * ROW_TILES, LANES)
    n_sub = t // CMP_STRIDE
    const = lambda shape: pl.BlockSpec(shape, lambda i: (0,) * len(shape))
    return pl.pallas_call(
        functools.partial(_compress_kernel, n_sub=n_sub),
        grid=(b,),
        in_specs=[pl.BlockSpec((1, t * ROW_TILES, LANES), lambda i: (i, 0, 0)),
                  const((CMP_LEN, 2, KV_W, KV_W)), const((CMP_LEN, 2, KV_W)), const((1, KV_W))],
        out_specs=[pl.BlockSpec((1, n_sub, KV_W), lambda i: (i, 0, 0)),
                   pl.BlockSpec((1, n_sub, KV_W), lambda i: (i, 0, 0))],
        out_shape=[jax.ShapeDtypeStruct((b, n_sub, KV_W), F32), jax.ShapeDtypeStruct((b, n_sub, KV_W), F32)],
        compiler_params=_cparams(("arbitrary",)),
        name="compress_prompt",
    )(cmp_rows, wbd, pos, kg)


def _softmax_rows(s, mask):
    s = jnp.where(mask, s, NEG)
    m = jnp.max(s, axis=-1, keepdims=True)
    e = jnp.exp(s - m)
    p = e / jnp.sum(e, axis=-1, keepdims=True)
    return jnp.where(mask, p, 0.0)


def _stack_heads(x, kv):
    return jnp.concatenate([x[:, (kv * GQA + g) * HEAD_DIM:(kv * GQA + g + 1) * HEAD_DIM] for g in range(GQA)],
                           axis=0)


def _block_scores_t(p, tq, n_cmp, n_blk, sel_mat_t):
    imp = p[0:tq]
    for g in range(1, GQA):
        imp = imp + p[g * tq:(g + 1) * tq]
    return _nt(sel_mat_t, imp, precision=HI)


def _top_rank_mask_t(score_t, n_blk):
    jidx = lax.broadcasted_iota(jnp.int32, score_t.shape, 0)
    cnt = jnp.zeros(score_t.shape, F32)
    for j in range(n_blk):
        rowv = score_t[j:j + 1, :]
        beats = (rowv > score_t) | ((rowv == score_t) & (j < jidx))
        cnt = cnt + beats.astype(F32)
    return cnt < float(N_SEL)


def _nsa_prompt_kernel(qn_ref, qr_ref, sm_ref, kc_ref, vc_ref, slc_ref, win_ref, selm_ref, exp_ref, o_ref,
                       selk_ref, *, t, n_cmp, n_blk):
    tq = Q_BLOCK
    n = pl.program_id(1)
    start = n * tq
    qn = qn_ref[0]
    qr = qr_ref[0]
    gates = _sigmoid(sm_ref[0])
    q_pos1 = start + lax.broadcasted_iota(jnp.int32, (tq, 1), 0)
    q_pos = jnp.concatenate([q_pos1] * GQA, axis=0)
    ci = lax.broadcasted_iota(jnp.int32, (1, n_cmp), 1)
    c_mask = (ci * CMP_STRIDE + (CMP_LEN - 1)) <= q_pos
    jcol = lax.broadcasted_iota(jnp.int32, (n_blk, 1), 0)
    qrow = start + lax.broadcasted_iota(jnp.int32, (1, tq), 1)
    cur = qrow // SLC_BLOCK
    forced_t = (jcol == 0) | (jcol == cur) | (jcol == cur - 1)
    valid_t = jcol * SLC_BLOCK <= qrow
    w_lo = jnp.maximum(start - WINDOW, 0)
    w_len = WINDOW + tq
    kp_w = w_lo + lax.broadcasted_iota(jnp.int32, (1, w_len), 1)
    w_mask = (kp_w <= q_pos) & (kp_w > q_pos - WINDOW)
    n_tiles = n + 1
    eye_q = (lax.broadcasted_iota(jnp.int32, (tq, tq), 0)
             == lax.broadcasted_iota(jnp.int32, (tq, tq), 1)).astype(BF16)

    outs = []
    for kv in range(N_KV):
        qn_s = _stack_heads(qn, kv).astype(BF16)
        qr_s = (_stack_heads(qr, kv) * ATTN_SCALE).astype(BF16)
        lanes = slice(kv * HEAD_DIM, (kv + 1) * HEAD_DIM)
        vlanes = slice(KV_W + kv * HEAD_DIM, KV_W + (kv + 1) * HEAD_DIM)
        kc = kc_ref[0][:, lanes].astype(BF16)
        vc = vc_ref[0][:, lanes].astype(BF16)
        p = _softmax_rows(_nt(qn_s, kc) * ATTN_SCALE, c_mask)
        o_cmp = _mm(p.astype(BF16), vc)
        score_t = _block_scores_t(p, tq, n_cmp, n_blk, selm_ref[...])
        score_t = jnp.where(forced_t, score_t + FORCE_BONUS, score_t)
        score_t = jnp.where(valid_t, score_t, NEG)
        sel_t = _top_rank_mask_t(score_t, n_blk) & valid_t
        sel = _nt(eye_q, jnp.where(sel_t, 1.0, 0.0).astype(BF16))
        selk_ref[...] = _mm(sel.astype(BF16), exp_ref[...])

        def tile_body(i, carry, lanes=lanes, vlanes=vlanes, qr_s=qr_s):
            m, l, acc = carry
            k0 = pl.multiple_of(i * tq, tq)
            kt = slc_ref[0, pl.ds(k0, tq), lanes].astype(BF16)
            vt = slc_ref[0, pl.ds(k0, tq), vlanes].astype(BF16)
            s = _nt(qr_s, kt)
            kp = k0 + lax.broadcasted_iota(jnp.int32, (1, tq), 1)
            sk = selk_ref[:, pl.ds(k0, tq)]
            ok = (jnp.concatenate([sk] * GQA, axis=0) > 0.5) & (kp <= q_pos)
            s = jnp.where(ok, s, NEG)
            m_new = jnp.maximum(m, jnp.max(s, axis=-1, keepdims=True))
            alpha = jnp.exp(m - m_new)
            e = jnp.where(ok, jnp.exp(s - m_new), 0.0)
            return (m_new, alpha * l + jnp.sum(e, axis=-1, keepdims=True),
                    alpha * acc + _mm(e.astype(BF16), vt))

        m0 = jnp.full((GQA * tq, 1), NEG, F32)
        l0 = jnp.zeros((GQA * tq, 1), F32)
        a0 = jnp.zeros((GQA * tq, HEAD_DIM), F32)
        _, l_s, acc_s = lax.fori_loop(0, n_tiles, tile_body, (m0, l0, a0))
        o_slc = acc_s / l_s
        kw = win_ref[0, pl.ds(pl.multiple_of(w_lo, tq), w_len), lanes].astype(BF16)
        vw = win_ref[0, pl.ds(pl.multiple_of(w_lo, tq), w_len), vlanes].astype(BF16)
        pw = _softmax_rows(_nt(qr_s, kw), w_mask)
        o_win = _mm(pw.astype(BF16), vw)
        for g in range(GQA):
            hd = kv * GQA + g
            rows = slice(g * tq, (g + 1) * tq)
            g0 = gates[:, SSM_HEADS + hd:SSM_HEADS + hd + 1]
            g1 = gates[:, SSM_HEADS + N_HEADS + hd:SSM_HEADS + N_HEADS + hd + 1]
            g2 = gates[:, SSM_HEADS + 2 * N_HEADS + hd:SSM_HEADS + 2 * N_HEADS + hd + 1]
            outs.append(g0 * o_cmp[rows] + g1 * o_slc[rows] + g2 * o_win[rows])
    o_ref[0] = jnp.concatenate(outs, axis=1)


def _selection_matrices(n_cmp, n_blk, n_keys):
    i = np.arange(n_cmp)[None, :]
    j = np.arange(n_blk)[:, None]
    sel = ((i >= SUB_PER_SLC * j - 1) & (i <= SUB_PER_SLC * j + SUB_PER_SLC - 1)).astype(np.float32)
    key = np.arange(n_keys)[None, :]
    expand = (key // SLC_BLOCK == j).astype(np.float32)
    return jnp.asarray(sel), jnp.asarray(expand, BF16)


def _nsa_prompt(qn3, qr3, u3, kc, vc, slc3, win3):
    b, t = qn3.shape[:2]
    n_cmp = t // CMP_STRIDE
    n_blk = t // SLC_BLOCK
    selm, expand = _selection_matrices(n_cmp, n_blk, t)
    selm = selm.at[:, n_cmp - 1].set(0.0)
    hw = N_HEADS * HEAD_DIM
    full = lambda w: pl.BlockSpec((1, t, w), lambda i, n: (i, 0, 0))
    return pl.pallas_call(
        functools.partial(_nsa_prompt_kernel, t=t, n_cmp=n_cmp, n_blk=n_blk),
        grid=(b, t // Q_BLOCK),
        in_specs=[pl.BlockSpec((1, Q_BLOCK, hw), lambda i, n: (i, n, 0)),
                  pl.BlockSpec((1, Q_BLOCK, hw), lambda i, n: (i, n, 0)),
                  pl.BlockSpec((1, Q_BLOCK, LANES), lambda i, n: (i, n, C_SMALL // LANES)),
                  pl.BlockSpec((1, n_cmp, KV_W), lambda i, n: (i, 0, 0)),
                  pl.BlockSpec((1, n_cmp, KV_W), lambda i, n: (i, 0, 0)),
                  full(ROW_W), full(ROW_W),
                  pl.BlockSpec((n_blk, n_cmp), lambda i, n: (0, 0)),
                  pl.BlockSpec((n_blk, t), lambda i, n: (0, 0))],
        out_specs=pl.BlockSpec((1, Q_BLOCK, hw), lambda i, n: (i, n, 0)),
        out_shape=jax.ShapeDtypeStruct((b, t, hw), F32),
        scratch_shapes=[pltpu.VMEM((Q_BLOCK, t), F32)],
        compiler_params=_cparams(("arbitrary", "arbitrary")),
        name="nsa_prompt",
    )(qn3, qr3, u3, kc, vc, slc3, win3, selm, expand)


N_PARTS = 4


def _nsa_sample_kernel(pt_ref, qn_ref, qr_ref, sm_ref, slcn_ref, winn_ref, wins_ref, wbd_ref, pos_ref, kg_ref,
                       selm_ref, cmp_hbm, slc_hbm, o_ref,
                       buf_ref, sem_ref, fs_ref, sel_ref, ocmp_ref, m_ref, l_ref, acc_ref,
                       *, past, ppp, n_blk_pad, batch):
    b = pl.program_id(0)
    s = pl.program_id(1)
    n_steps = 2 * N_PARTS
    c = b * n_steps + s
    total = batch * n_steps
    rp = ppp * PAGE_SIZE
    nsp = rp // CMP_STRIDE
    n_sub = N_PARTS * nsp
    win_buf = wins_ref.shape[1]
    cur = past // SLC_BLOCK

    def page_copy(hbm, bb, part, pg, slot):
        tiles = PAGE_SIZE * ROW_TILES
        return pltpu.make_async_copy(hbm.at[pt_ref[bb, part * ppp + pg]],
                                     buf_ref.at[slot, pl.ds(pg * tiles, tiles)], sem_ref.at[slot])

    def load_rows(first, count, stride, tile):
        return buf_ref[slot, pl.ds(first * ROW_TILES + tile, count, stride=stride * ROW_TILES), :]

    def start_chunk(cc, slot):
        bb = cc // n_steps
        ss = cc % n_steps

        @pl.when(ss < N_PARTS)
        def _():
            for pg in range(ppp):
                page_copy(cmp_hbm, bb, ss, pg, slot).start()

        @pl.when(ss >= N_PARTS)
        def _():
            for pg in range(ppp):
                page_copy(slc_hbm, bb, ss - N_PARTS, pg, slot).start()

    @pl.when(c == 0)
    def _():
        start_chunk(c, 0)

    @pl.when(c + 1 < total)
    def _():
        start_chunk(c + 1, (c + 1) % 2)

    slot = c % 2
    for pg in range(ppp):
        page_copy(cmp_hbm, b, 0, pg, slot).wait()

    def stack_q(ref, kv, scale):
        rows = [ref[0, :, (kv * GQA + g) * HEAD_DIM:(kv * GQA + g + 1) * HEAD_DIM] for g in range(GQA)]
        return jnp.concatenate(rows + [jnp.zeros((8 - GQA, HEAD_DIM), F32)], axis=0) * scale

    @pl.when(s < N_PARTS)
    def _():
        acc = _sub_block_proj(lambda l, j: load_rows(l, nsp, CMP_STRIDE, j), wbd_ref, pos_ref)
        row0 = pl.multiple_of(s * nsp, nsp)
        fs_ref[0, pl.ds(row0, nsp), :] = acc[0][0]
        fs_ref[1, pl.ds(row0, nsp), :] = acc[0][1]
        fs_ref[2, pl.ds(row0, nsp), :] = acc[1][0]
        fs_ref[3, pl.ds(row0, nsp), :] = acc[1][1]

    @pl.when(s == N_PARTS - 1)
    def _():
        kc, vc = _compress_finish(fs_ref[0], fs_ref[1], fs_ref[2], fs_ref[3], kg_ref[...])
        ci = lax.broadcasted_iota(jnp.int32, (1, n_sub), 1)
        c_mask = (ci * CMP_STRIDE + (CMP_LEN - 1)) <= past
        imps = []
        for kv in range(N_KV):
            lanes = slice(kv * HEAD_DIM, (kv + 1) * HEAD_DIM)
            q = stack_q(qn_ref, kv, 1.0).astype(BF16)
            p = _softmax_rows(_nt(q, kc[:, lanes].astype(BF16)) * ATTN_SCALE, c_mask)
            ocmp_ref[kv] = _mm(p.astype(BF16), vc[:, lanes].astype(BF16))
            imps.append(p[0:1] + p[1:2] + p[2:3] + p[3:4])
        imp = jnp.concatenate(imps + [jnp.zeros((8 - N_KV, n_sub), F32)], axis=0)
        score = _mm(imp, selm_ref[...], precision=HI)
        jrow = lax.broadcasted_iota(jnp.int32, (1, n_blk_pad), 1)
        forced = (jrow == 0) | (jrow == cur) | (jrow == cur - 1)
        valid = jrow * SLC_BLOCK <= past
        score = jnp.where(forced, score + FORCE_BONUS, score)
        score = jnp.where(valid, score, NEG)
        eye = (lax.broadcasted_iota(jnp.int32, (n_blk_pad, n_blk_pad), 0)
               == lax.broadcasted_iota(jnp.int32, (n_blk_pad, n_blk_pad), 1)).astype(F32)
        score_c = _nt(eye, score, precision=HI)
        jc = lax.broadcasted_iota(jnp.int32, (n_blk_pad, n_blk_pad), 0)
        jr = lax.broadcasted_iota(jnp.int32, (n_blk_pad, n_blk_pad), 1)
        sels = []
        for kv in range(N_KV):
            colv = score_c[:, kv:kv + 1]
            rowv = score[kv:kv + 1, :]
            beats = (colv > rowv) | ((colv == rowv) & (jc < jr))
            rank = jnp.sum(beats.astype(F32), axis=0, keepdims=True)
            sels.append(jnp.where((rank < float(N_SEL)) & valid, 1.0, 0.0))
        sel_ref[...] = jnp.concatenate(sels + [jnp.zeros((8 - N_KV, n_blk_pad), F32)], axis=0)

    @pl.when(s >= N_PARTS)
    def _():
        part = s - N_PARTS

        @pl.when(part == 0)
        def _():
            m_ref[...] = jnp.full(m_ref.shape, NEG, F32)
            l_ref[...] = jnp.zeros(l_ref.shape, F32)
            acc_ref[...] = jnp.zeros(acc_ref.shape, F32)

        key = part * rp + lax.broadcasted_iota(jnp.int32, (n_blk_pad, rp), 1)
        blk = lax.broadcasted_iota(jnp.int32, (n_blk_pad, rp), 0)
        expand = jnp.where(key // SLC_BLOCK == blk, 1.0, 0.0).astype(BF16)
        selk = _mm(sel_ref[...].astype(BF16), expand)
        for kv in range(N_KV):
            q = stack_q(qr_ref, kv, ATTN_SCALE).astype(BF16)
            off = (kv * HEAD_DIM) % LANES
            kt = load_rows(0, rp, 1, (kv * HEAD_DIM) // LANES)[:, off:off + HEAD_DIM].astype(BF16)
            vt = load_rows(0, rp, 1, (KV_W + kv * HEAD_DIM) // LANES)[:, off:off + HEAD_DIM].astype(BF16)
            ok = selk[kv:kv + 1, :] > 0.5
            sc = jnp.where(ok, _nt(q, kt), NEG)
            m_old = m_ref[kv]
            m_new = jnp.maximum(m_old, jnp.max(sc, axis=-1, keepdims=True))
            alpha = jnp.exp(m_old - m_new)
            e = jnp.where(ok, jnp.exp(sc - m_new), 0.0)
            m_ref[kv] = m_new
            l_ref[kv] = alpha * l_ref[kv] + jnp.sum(e, axis=-1, keepdims=True)
            acc_ref[kv] = alpha * acc_ref[kv] + _mm(e.astype(BF16), vt)

    @pl.when(s == n_steps - 1)
    def _():
        gates = _sigmoid(sm_ref[0])
        kpos = past - win_buf + lax.broadcasted_iota(jnp.int32, (1, win_buf), 1)
        w_mask = (kpos <= past) & (kpos > past - WINDOW) & (kpos >= 0)
        outs = []
        for kv in range(N_KV):
            lanes = slice(kv * HEAD_DIM, (kv + 1) * HEAD_DIM)
            vlanes = slice(KV_W + kv * HEAD_DIM, KV_W + (kv + 1) * HEAD_DIM)
            q = stack_q(qr_ref, kv, ATTN_SCALE)
            qb = q.astype(BF16)
            k_new = slcn_ref[0, :, lanes].astype(BF16).astype(F32)
            v_new = slcn_ref[0, :, vlanes].astype(BF16).astype(F32)
            ok_new = sel_ref[kv:kv + 1, cur:cur + 1] > 0.5
            s_new = jnp.where(ok_new, jnp.sum(qb.astype(F32) * k_new, axis=-1, keepdims=True), NEG)
            m_old = m_ref[kv]
            m_new = jnp.maximum(m_old, s_new)
            alpha = jnp.exp(m_old - m_new)
            e_new = jnp.where(ok_new, jnp.exp(s_new - m_new), 0.0)
            l_s = alpha * l_ref[kv] + e_new
            o_slc = (alpha * acc_ref[kv] + e_new.astype(BF16).astype(F32) * v_new) / l_s
            kw = wins_ref[0, :, lanes].astype(BF16)
            vw = wins_ref[0, :, vlanes].astype(BF16)
            sw = jnp.where(w_mask, _nt(qb, kw), NEG)
            kw_new = winn_ref[0, :, lanes].astype(BF16).astype(F32)
            vw_new = winn_ref[0, :, vlanes].astype(BF16).astype(F32)
            sw_new = jnp.sum(qb.astype(F32) * kw_new, axis=-1, keepdims=True)
            mw = jnp.maximum(jnp.max(sw, axis=-1, keepdims=True), sw_new)
            ew = jnp.where(w_mask, jnp.exp(sw - mw), 0.0)
            ew_new = jnp.exp(sw_new - mw)
            lw = jnp.sum(ew, axis=-1, keepdims=True) + ew_new
            pw = ew / lw
            o_win = _mm(pw.astype(BF16), vw) + (ew_new / lw).astype(BF16).astype(F32) * vw_new
            o_cmp = ocmp_ref[kv]
            for g in range(GQA):
                hd = kv * GQA + g
                g0 = gates[:, SSM_HEADS + hd:SSM_HEADS + hd + 1]
                g1 = gates[:, SSM_HEADS + N_HEADS + hd:SSM_HEADS + N_HEADS + hd + 1]
                g2 = gates[:, SSM_HEADS + 2 * N_HEADS + hd:SSM_HEADS + 2 * N_HEADS + hd + 1]
                outs.append(g0 * o_cmp[g:g + 1] + g1 * o_slc[g:g + 1] + g2 * o_win[g:g + 1])
        o_ref[0] = jnp.concatenate(outs, axis=1)


def _nsa_sample(qn, qr, u2d, slc_new, win_new, win_state, page_table, cache_cmp, cache_slc, wbd, pos, kg):
    b = qn.shape[0]
    n_pages = page_table.shape[1]
    past = n_pages * PAGE_SIZE
    ppp = n_pages // N_PARTS
    n_sub = past // CMP_STRIDE
    n_blk = past // SLC_BLOCK + 1
    n_blk_pad = -(-n_blk // LANES) * LANES
    selm, _ = _selection_matrices(n_sub, n_blk_pad, 1)
    selm = selm.at[:, n_sub - 1].set(0.0).T
    hw = N_HEADS * HEAD_DIM
    win_buf = win_state.shape[1]
    n_steps = 2 * N_PARTS
    per_b = lambda w: pl.BlockSpec((1, 1, w), lambda i, s, pt: (i, 0, 0))
    const = lambda shape: pl.BlockSpec(shape, lambda i, s, pt: (0,) * len(shape))
    grid_spec = pltpu.PrefetchScalarGridSpec(
        num_scalar_prefetch=1,
        grid=(b, n_steps),
        in_specs=[per_b(hw), per_b(hw),
                  pl.BlockSpec((1, 1, LANES), lambda i, s, pt: (i, 0, C_SMALL // LANES)),
                  per_b(ROW_W), per_b(ROW_W),
                  pl.BlockSpec((1, win_buf, ROW_W), lambda i, s, pt: (i, 0, 0)),
                  const((CMP_LEN, 2, KV_W, KV_W)), const((CMP_LEN, 2, KV_W)), const((1, KV_W)),
                  const((n_sub, n_blk_pad)),
                  pl.BlockSpec(memory_space=pl.ANY), pl.BlockSpec(memory_space=pl.ANY)],
        out_specs=per_b(hw),
        scratch_shapes=[pltpu.VMEM((2, ppp * PAGE_SIZE * ROW_TILES, LANES), F32), pltpu.SemaphoreType.DMA((2,)),
                        pltpu.VMEM((4, n_sub, KV_W), F32), pltpu.VMEM((8, n_blk_pad), F32),
                        pltpu.VMEM((N_KV, 8, HEAD_DIM), F32), pltpu.VMEM((N_KV, 8, 1), F32),
                        pltpu.VMEM((N_KV, 8, 1), F32), pltpu.VMEM((N_KV, 8, HEAD_DIM), F32)])
    return pl.pallas_call(
        functools.partial(_nsa_sample_kernel, past=past, ppp=ppp, n_blk_pad=n_blk_pad, batch=b),
        grid_spec=grid_spec,
        out_shape=jax.ShapeDtypeStruct((b, 1, hw), F32),
        compiler_params=_cparams(("arbitrary", "arbitrary")),
        name="nsa_sample",
    )(page_table, qn.reshape(b, 1, hw), qr.reshape(b, 1, hw), u2d.reshape(b, 1, D_IN_P),
      slc_new.reshape(b, 1, ROW_W), win_new.reshape(b, 1, ROW_W), win_state, wbd, pos, kg, selm,
      cache_cmp, cache_slc).reshape(b, hw)


def _merge_kernel(x_ref, y_ref, o_ref, mg_ref, wps_ref, wpa_ref, wo_ref, n2_ref, wpq_ref, x2_ref, h2_ref, pq_ref):
    mg = _sigmoid(mg_ref[...])
    mixed = (mg[:, :D_MODEL] * _mm(y_ref[...].astype(BF16), wps_ref[...])
             + mg[:, D_MODEL:] * _mm(o_ref[...].astype(BF16), wpa_ref[...]))
    x2 = x_ref[...] + _mm(mixed.astype(BF16), wo_ref[...])
    x2_ref[...] = x2
    h2 = x2 * lax.rsqrt(jnp.mean(x2 * x2, axis=-1, keepdims=True) + EPS) * n2_ref[...]
    h2_ref[...] = h2
    pq_ref[...] = _mm(h2.astype(BF16), wpq_ref[...])


def _merge(x2d, y_ssm, o_attn, u2d, wps, wpa, wo, norm2_w, wpq):
    n = x2d.shape[0]
    tm = min(256, n)
    rows = lambda w, cb=0: pl.BlockSpec((tm, w), lambda i: (i, cb))
    const = lambda shape: pl.BlockSpec(shape, lambda i: (0, 0))
    out = jax.ShapeDtypeStruct((n, D_MODEL), F32)
    return pl.pallas_call(
        _merge_kernel,
        grid=(n // tm,),
        in_specs=[rows(D_MODEL), rows(D_INNER), rows(N_HEADS * HEAD_DIM), rows(2 * D_MODEL, C_MERGE // (2 * D_MODEL)),
                  const((D_INNER, D_MODEL)), const((N_HEADS * HEAD_DIM, D_MODEL)), const((D_MODEL, D_MODEL)),
                  const((1, D_MODEL)), const((D_MODEL, PEER_HEADS * PEER_DK))],
        out_specs=[rows(D_MODEL), rows(D_MODEL), rows(PEER_HEADS * PEER_DK)],
        out_shape=[out, out, jax.ShapeDtypeStruct((n, PEER_HEADS * PEER_DK), F32)],
        compiler_params=_cparams(("arbitrary",)),
        name="merge",
    )(x2d, y_ssm, o_attn, u2d, wps, wpa, wo, norm2_w.reshape(1, D_MODEL), wpq)


def _top_k_rows(s, k):
    r = s.shape[0]
    ridx = lax.broadcasted_iota(jnp.int32, s.shape, 0)
    vals, idxs = [], []
    for _ in range(k):
        m = jnp.max(s, axis=0, keepdims=True)
        idx = jnp.min(jnp.where(s == m, ridx, r), axis=0, keepdims=True)
        vals.append(m)
        idxs.append(idx)
        s = jnp.where(ridx == idx, -jnp.inf, s)
    return jnp.concatenate(vals, axis=0), jnp.concatenate(idxs, axis=0)


def _peer_select_kernel(pq_ref, sk_ref, e_ref, g_ref):
    half = PEER_DK // 2
    es, gs = [], []
    for hd in range(PEER_HEADS):
        tops = []
        for cidx in range(2):
            col = (hd * 2 + cidx) * half
            q = pq_ref[:, col:col + half].astype(BF16)
            tops.append(_top_k_rows(_nt(sk_ref[hd, cidx], q), PEER_TOPK))
        (s1, i1), (s2, i2) = tops
        cand = jnp.concatenate([s1[a:a + 1] + s2 for a in range(PEER_TOPK)], axis=0)
        top_s, top_c = _top_k_rows(cand, PEER_TOPK)
        a_code = top_c // PEER_TOPK
        b_code = top_c % PEER_TOPK
        i1s = jnp.zeros_like(top_c)
        i2s = jnp.zeros_like(top_c)
        for a in range(PEER_TOPK):
            i1s = jnp.where(a_code == a, i1[a:a + 1], i1s)
            i2s = jnp.where(b_code == a, i2[a:a + 1], i2s)
        es.append(i1s * N_KEYS + i2s)
        ex = jnp.exp(top_s - top_s[0:1])
        gs.append(ex / jnp.sum(ex, axis=0, keepdims=True))
    e_ref[...] = jnp.concatenate(es, axis=0).T
    g_ref[...] = jnp.concatenate(gs, axis=0).T


def _peer_select(pq, sub_keys_bf):
    n = pq.shape[0]
    tn = min(256, n)
    nk = PEER_HEADS * PEER_TOPK
    return pl.pallas_call(
        _peer_select_kernel,
        grid=(n // tn,),
        in_specs=[pl.BlockSpec((tn, PEER_HEADS * PEER_DK), lambda i: (i, 0)),
                  pl.BlockSpec((PEER_HEADS, 2, N_KEYS, PEER_DK // 2), lambda i: (0, 0, 0, 0))],
        out_specs=[pl.BlockSpec((tn, nk), lambda i: (i, 0)), pl.BlockSpec((tn, nk), lambda i: (i, 0))],
        out_shape=[jax.ShapeDtypeStruct((n, nk), jnp.int32), jax.ShapeDtypeStruct((n, nk), F32)],
        compiler_params=_cparams(("arbitrary",)),
        name="peer_select",
    )(pq, sub_keys_bf)


PEER_TOK = 8


def _peer_gather_kernel(e_ref, en_ref, g_ref, h_ref, x_ref, uv_hbm, o_ref, buf_ref, sem_ref, *, n):
    i = pl.program_id(0)
    nk = PEER_HEADS * PEER_TOPK
    rows = PEER_TOK * nk

    def issue(idx_ref, slot):
        def tok_body(t, carry):
            for k in range(nk):
                pltpu.make_async_copy(uv_hbm.at[pl.ds(idx_ref[t, k], 1)],
                                      buf_ref.at[slot, pl.ds(t * nk + k, 1)], sem_ref.at[slot]).start()
            return carry
        lax.fori_loop(0, PEER_TOK, tok_body, 0)

    @pl.when(i == 0)
    def _():
        issue(e_ref, 0)

    @pl.when(i + 1 < n)
    def _():
        issue(en_ref, (i + 1) % 2)

    slot = i % 2
    pltpu.make_async_copy(uv_hbm.at[pl.ds(0, rows)], buf_ref.at[slot], sem_ref.at[slot]).wait()

    eye = (lax.broadcasted_iota(jnp.int32, (nk, nk), 0) == lax.broadcasted_iota(jnp.int32, (nk, nk), 1)).astype(F32)
    g_t = _nt(eye, g_ref[...], precision=HI)
    outs = []
    for t in range(PEER_TOK):
        u = buf_ref[slot, t * nk:(t + 1) * nk, :D_MODEL]
        v = buf_ref[slot, t * nk:(t + 1) * nk, D_MODEL:]
        pre = jnp.sum(u * h_ref[t:t + 1, :], axis=-1, keepdims=True)
        act = 0.5 * pre * (1.0 + lax.erf(pre * np.float32(math.sqrt(0.5))))
        outs.append(jnp.sum((g_t[:, t:t + 1] * act) * v, axis=0, keepdims=True))
    o_ref[...] = x_ref[...] + jnp.concatenate(outs, axis=0)


def _peer_gather(e, g, h2, x2, uv):
    n = h2.shape[0]
    nk = PEER_HEADS * PEER_TOPK
    steps = n // PEER_TOK
    blk = lambda w: pl.BlockSpec((PEER_TOK, w), lambda i: (i, 0))
    smem = lambda imap: pl.BlockSpec((PEER_TOK, nk), imap, memory_space=pltpu.SMEM)
    return pl.pallas_call(
        functools.partial(_peer_gather_kernel, n=steps),
        grid=(steps,),
        in_specs=[smem(lambda i: (i, 0)), smem(lambda i: (jnp.minimum(i + 1, steps - 1), 0)),
                  blk(nk), blk(D_MODEL), blk(D_MODEL), pl.BlockSpec(memory_space=pl.ANY)],
        out_specs=blk(D_MODEL),
        out_shape=jax.ShapeDtypeStruct((n, D_MODEL), F32),
        scratch_shapes=[pltpu.VMEM((2, PEER_TOK * nk, 2 * D_MODEL), F32), pltpu.SemaphoreType.DMA((2,))],
        compiler_params=_cparams(("arbitrary",)),
        name="peer_gather",
    )(e, e, g, h2, x2, uv)


def _permute_w_in(w_in):
    sizes = (D_INNER, CONV_DIM, SSM_HEADS, N_HEADS * HEAD_DIM, 3 * ROW_W, 3 * N_HEADS, 2 * D_MODEL)
    cuts = np.cumsum((0,) + sizes)
    z, xbc, dt, q, kv, ng, mg = (w_in[:, cuts[i]:cuts[i + 1]] for i in range(7))
    pad = jnp.zeros((D_MODEL, LANES - SSM_HEADS - 3 * N_HEADS), w_in.dtype)
    return jnp.concatenate([xbc, z, mg, q, kv, dt, ng, pad], axis=1).astype(BF16)


def _layer_weights(norm1_w, w_in, conv_w, conv_b, dt_bias, a_log, d_skip, ssm_norm_w, w_proj_ssm, q_norm_w,
                   k_norm_w, cmp_pos, w_cmp, w_proj_attn, w_out, norm2_w, w_pq, sub_keys, expert_u, expert_v):
    wbd, pos, kg = _compress_weights(w_cmp, cmp_pos, k_norm_w[0])
    return dict(norm1_w=norm1_w, w_in=_permute_w_in(w_in), conv_w=conv_w, conv_b=conv_b, dt_bias=dt_bias,
                a_log=a_log, d_skip=d_skip, ssm_norm_w=ssm_norm_w, wps=w_proj_ssm.astype(BF16),
                q_norm_w=q_norm_w, k_norm_w=k_norm_w, wbd=wbd, pos=pos, kg=kg, wpa=w_proj_attn.astype(BF16),
                wo=w_out.astype(BF16), norm2_w=norm2_w, wpq=w_pq.astype(BF16), sub_keys=sub_keys.astype(BF16),
                uv=jnp.concatenate([expert_u, expert_v], axis=1))


def _token_tail(x2d, y_ssm, o_attn, u2d, w):
    x2, h2, pq = _merge(x2d, y_ssm, o_attn, u2d, w["wps"], w["wpa"], w["wo"], w["norm2_w"], w["wpq"])
    e, g = _peer_select(pq, w["sub_keys"])
    return _peer_gather(e, g, h2, x2, w["uv"])


def _layer_prompt(x, w, win_buf):
    b, t = x.shape[:2]
    assert t % Q_BLOCK == 0 and t >= WINDOW + Q_BLOCK and t >= win_buf
    x2d = x.reshape(b * t, D_MODEL)
    u2d = _inproj(x2d, w["norm1_w"], w["w_in"])
    u3 = u2d.reshape(b, t, D_IN_P)
    y_ssm, h_last = _ssd(u3, jnp.zeros((b, D_INNER, D_STATE), F32), jnp.zeros((b, 8, CONV_DIM), F32),
                         w["conv_w"], w["conv_b"], w["dt_bias"], w["a_log"], w["d_skip"], w["ssm_norm_w"])
    cos_t, sin_t = _rope_tables(jnp.arange(t, dtype=jnp.int32))
    qn, qr, slc, win = _nsa_prep(u2d, cos_t, sin_t, w["q_norm_w"], w["k_norm_w"], t)
    cmp_rows = u3[:, :, C_KV:C_KV + ROW_W]
    kc, vc = _compress_prompt(cmp_rows, w["wbd"], w["pos"], w["kg"])
    hw = N_HEADS * HEAD_DIM
    o_attn = _nsa_prompt(qn.reshape(b, t, hw), qr.reshape(b, t, hw), u3, kc, vc,
                         slc.reshape(b, t, ROW_W), win.reshape(b, t, ROW_W))
    y = _token_tail(x2d, y_ssm.reshape(b * t, D_INNER), o_attn.reshape(b * t, hw), u2d, w)
    rows = lambda a: a.reshape(b, -1, 2, N_KV, HEAD_DIM)
    state = (rows(cmp_rows), rows(slc), rows(win.reshape(b, t, ROW_W)[:, t - win_buf:]),
             h_last.reshape(b, SSM_HEADS, SSM_HEADDIM, D_STATE), u3[:, t - (CONV_W - 1):, C_XBC:C_XBC + CONV_DIM])
    return y.reshape(b, t, D_MODEL), state


def _layer_sample(x, w, cache_cmp, cache_slc, page_table, win_state, h0, conv_state):
    b, t = x.shape[:2]
    assert t == 1
    past = page_table.shape[1] * PAGE_SIZE
    win_buf = win_state.shape[1]
    assert page_table.shape[1] % N_PARTS == 0 and win_buf >= 1
    x2d = x.reshape(b, D_MODEL)
    u2d = _inproj(x2d, w["norm1_w"], w["w_in"])
    u3 = u2d.reshape(b, 1, D_IN_P)
    prefix8 = jnp.pad(conv_state, ((0, 0), (8 - (CONV_W - 1), 0), (0, 0)))
    y_ssm, h_last = _ssd(u3, h0.reshape(b, D_INNER, D_STATE), prefix8, w["conv_w"], w["conv_b"], w["dt_bias"],
                         w["a_log"], w["d_skip"], w["ssm_norm_w"])
    cos_t, sin_t = _rope_tables(jnp.full((1,), past, jnp.int32))
    qn, qr, slc, win = _nsa_prep(u2d, cos_t, sin_t, w["q_norm_w"], w["k_norm_w"], 1)
    n_pool = cache_cmp.shape[0]
    o_attn = _nsa_sample(qn, qr, u2d, slc, win, win_state.reshape(b, win_buf, ROW_W), page_table,
                         cache_cmp.reshape(n_pool, PAGE_SIZE * ROW_TILES, LANES),
                         cache_slc.reshape(n_pool, PAGE_SIZE * ROW_TILES, LANES),
                         w["wbd"], w["pos"], w["kg"])
    y = _token_tail(x2d, y_ssm.reshape(b, D_INNER), o_attn, u2d, w)
    rows = lambda a: a.reshape(b, -1, 2, N_KV, HEAD_DIM)
    new_win = jnp.concatenate([win_state.reshape(b, win_buf, ROW_W), win.reshape(b, 1, ROW_W)], axis=1)[:, 1:]
    new_conv = jnp.concatenate([conv_state, u3[:, :, C_XBC:C_XBC + CONV_DIM]], axis=1)[:, 1:]
    state = (rows(u2d[:, C_KV:C_KV + ROW_W]), rows(slc), rows(new_win),
             h_last.reshape(b, SSM_HEADS, SSM_HEADDIM, D_STATE), new_conv)
    return y.reshape(b, 1, D_MODEL), state


def kernel(x_prompt, x_sample, cache_cmp_kv, cache_slc_kv, page_table, state_win_kv, state_ssm, state_conv,
           norm1_w, w_in, conv_w, conv_b, dt_bias, a_log, d_skip, ssm_norm_w, w_proj_ssm, q_norm_w, k_norm_w,
           cmp_pos, w_cmp, w_proj_attn, w_out, norm2_w, w_pq, sub_keys, expert_u, expert_v):
    depth = w_in.shape[0]
    win_buf = state_win_kv.shape[2]
    yp, ys = x_prompt, x_sample
    new_p, new_s = [], []
    for l in range(depth):
        w = _layer_weights(norm1_w[l], w_in[l], conv_w[l], conv_b[l], dt_bias[l], a_log[l], d_skip[l],
                           ssm_norm_w[l], w_proj_ssm[l], q_norm_w[l], k_norm_w[l], cmp_pos[l], w_cmp[l],
                           w_proj_attn[l], w_out[l], norm2_w[l], w_pq[l], sub_keys[l], expert_u[l], expert_v[l])
        yp, st_p = _layer_prompt(yp, w, win_buf)
        ys, st_s = _layer_sample(ys, w, cache_cmp_kv[l], cache_slc_kv[l], page_table, state_win_kv[l],
                                 state_ssm[l], state_conv[l])
        new_p.append(st_p)
        new_s.append(st_s)
    stack = lambda states, i: jnp.stack([s[i] for s in states], axis=0)
    return (yp, ys, stack(new_p, 0), stack(new_s, 0), stack(new_p, 1), stack(new_s, 1), stack(new_p, 2),
            stack(new_s, 2), stack(new_p, 3), stack(new_s, 3), stack(new_p, 4), stack(new_s, 4))
```

```python
import functools
import math

import jax
import jax.numpy as jnp
import numpy as np
from jax import lax
from jax.experimental import pallas as pl
from jax.experimental.pallas import tpu as pltpu

F32 = jnp.float32
BF16 = jnp.bfloat16
HI = lax.Precision.HIGHEST

D_MODEL = 1024
PAGE_SIZE = 128
D_INNER = 2048
SSM_HEADDIM = 64
SSM_HEADS = 32
SSM_GROUPS = 8
SSM_HPG = 4
D_STATE = 128
CONV_W = 4
CONV_DIM = 4096
SSD_CHUNK = 128
N_HEADS = 16
N_KV = 4
GQA = 4
HEAD_DIM = 64
CMP_STRIDE = 16
CMP_LEN = 32
SLC_BLOCK = 64
SUB_PER_SLC = 4
N_SEL = 16
WINDOW = 512
Q_BLOCK = 128
ROPE_THETA = 10000.0
ATTN_SCALE = HEAD_DIM ** -0.5
PEER_HEADS = 8
N_KEYS = 128
PEER_TOPK = 16
PEER_DK = 128
EPS = 1e-6
NEG = -1e9
FORCE_BONUS = 1e3

LANES = 128
KV_W = N_KV * HEAD_DIM
ROW_W = 2 * KV_W
ROW_TILES = ROW_W // LANES
C_XBC, C_Z, C_MERGE, C_Q, C_KV, C_SMALL = 0, 4096, 6144, 8192, 9216, 10752
D_IN_P = 10880
IN_TILE_N = 2176
VMEM_LIMIT = 56 * 1024 * 1024


def _cparams(sem, vmem=VMEM_LIMIT):
    return pltpu.CompilerParams(dimension_semantics=sem, vmem_limit_bytes=vmem)


def _nt(a, b, precision=None):
    return lax.dot_general(a, b, (((1,), (1,)), ((), ())), preferred_element_type=F32, precision=precision)


def _tn(a, b, precision=None):
    return lax.dot_general(a, b, (((0,), (0,)), ((), ())), preferred_element_type=F32, precision=precision)


def _mm(a, b, precision=None):
    return jnp.dot(a, b, preferred_element_type=F32, precision=precision)


def _sigmoid(x):
    return 1.0 / (1.0 + jnp.exp(-x))


def _silu(x):
    return x * _sigmoid(x)


def _softplus(x):
    return jnp.maximum(x, 0.0) + jnp.log1p(jnp.exp(-jnp.abs(x)))


def _inproj_kernel(x_ref, nw_ref, w_ref, o_ref):
    x = x_ref[...]
    h = x * lax.rsqrt(jnp.mean(x * x, axis=-1, keepdims=True) + EPS) * nw_ref[...]
    o_ref[...] = _mm(h.astype(BF16), w_ref[...])


def _inproj(x2d, norm_w, w_p):
    n = x2d.shape[0]
    tm = min(512, n)
    return pl.pallas_call(
        _inproj_kernel,
        grid=(D_IN_P // IN_TILE_N, n // tm),
        in_specs=[pl.BlockSpec((tm, D_MODEL), lambda j, i: (i, 0)),
                  pl.BlockSpec((1, D_MODEL), lambda j, i: (0, 0)),
                  pl.BlockSpec((D_MODEL, IN_TILE_N), lambda j, i: (0, j))],
        out_specs=pl.BlockSpec((tm, IN_TILE_N), lambda j, i: (i, j)),
        out_shape=jax.ShapeDtypeStruct((n, D_IN_P), F32),
        compiler_params=_cparams(("arbitrary", "arbitrary")),
        name="inproj",
    )(x2d, norm_w.reshape(1, D_MODEL), w_p)


def _ssd_kernel(xbc_ref, z_ref, sm_ref, h0_ref, pre_ref, cw_ref, cb_ref, dtb_ref, alog_ref, dsk_ref, nw_ref,
                y_ref, hl_ref, st_ref, xe_ref, *, q, qp):
    c = pl.program_id(1)

    @pl.when(c == 0)
    def _():
        st_ref[...] = h0_ref[0]
        xe_ref[0:8, :] = pre_ref[0]

    @pl.when(c > 0)
    def _():
        xe_ref[0:8, :] = xe_ref[qp:qp + 8, :]

    xe_ref[8:8 + q, :] = xbc_ref[0]
    if qp > q:
        xe_ref[8 + q:8 + qp, :] = jnp.zeros((qp - q, CONV_DIM), F32)

    conv = cb_ref[...] + xe_ref[8:8 + qp, :] * cw_ref[3:4, :]
    for s in range(1, CONV_W):
        conv = conv + xe_ref[8 - s:8 - s + qp, :] * cw_ref[3 - s:4 - s, :]
    act = _silu(conv)

    lane = lax.broadcasted_iota(jnp.int32, (1, LANES), 1)
    head_lane = lane < SSM_HEADS
    sm = sm_ref[0]
    if qp > q:
        sm = jnp.concatenate([sm, jnp.zeros((qp - q, LANES), F32)], axis=0)
    dt = _softplus(sm + dtb_ref[...])
    dt = jnp.where(head_lane, dt, 0.0)
    if qp > q:
        row = lax.broadcasted_iota(jnp.int32, (qp, 1), 0)
        dt = jnp.where(row < q, dt, 0.0)
    a = jnp.where(head_lane, -jnp.exp(alog_ref[...]), 0.0)
    da = dt * a
    ri = lax.broadcasted_iota(jnp.int32, (qp, qp), 0)
    ci = lax.broadcasted_iota(jnp.int32, (qp, qp), 1)
    causal = ci <= ri
    cs = _mm(causal.astype(F32), da, precision=HI)
    eye = (lax.broadcasted_iota(jnp.int32, (LANES, LANES), 0)
           == lax.broadcasted_iota(jnp.int32, (LANES, LANES), 1)).astype(F32)
    cs_t = _nt(eye, cs, precision=HI)
    cs_last = cs[qp - 1:qp, :]
    e_cs = jnp.exp(cs)
    e_end = jnp.exp(cs_last - cs)
    e_last = jnp.exp(cs_last)

    for g in range(SSM_GROUPS):
        bm = act[:, D_INNER + g * D_STATE:D_INNER + (g + 1) * D_STATE].astype(BF16)
        cm = act[:, D_INNER + SSM_GROUPS * D_STATE + g * D_STATE:
                 D_INNER + SSM_GROUPS * D_STATE + (g + 1) * D_STATE].astype(BF16)
        cbm = _nt(cm, bm)
        r0 = g * SSM_HPG * SSM_HEADDIM
        s_g = st_ref[r0:r0 + SSM_HPG * SSM_HEADDIM, :]
        y_off = _nt(cm, s_g.astype(BF16))
        ys, xds, decs = [], [], []
        for r in range(SSM_HPG):
            h = g * SSM_HPG + r
            xs = act[:, h * SSM_HEADDIM:(h + 1) * SSM_HEADDIM]
            col = cs[:, h:h + 1]
            rowv = cs_t[h:h + 1, :]
            lmat = jnp.exp(jnp.where(causal, col - rowv, -jnp.inf))
            xdt = xs * dt[:, h:h + 1]
            y_d = _mm((cbm * lmat).astype(BF16), xdt.astype(BF16))
            ys.append(y_d + y_off[:, r * SSM_HEADDIM:(r + 1) * SSM_HEADDIM] * e_cs[:, h:h + 1]
                      + xs * dsk_ref[:, h:h + 1])
            xds.append(xdt * e_end[:, h:h + 1])
            decs.append(jnp.broadcast_to(e_last[:, h:h + 1], (SSM_HEADDIM, 1)))
        xd = jnp.concatenate(xds, axis=1)
        new = _tn(xd.astype(BF16), bm)
        dec = jnp.concatenate(decs, axis=0)
        st_ref[r0:r0 + SSM_HPG * SSM_HEADDIM, :] = dec * s_g + new
        yg = jnp.concatenate(ys, axis=1)
        w = D_INNER // SSM_GROUPS
        zg = z_ref[0][:, g * w:(g + 1) * w]
        if qp > q:
            yg = yg[:q]
        yg = yg * _silu(zg)
        yg = yg * lax.rsqrt(jnp.mean(yg * yg, axis=-1, keepdims=True) + EPS) * nw_ref[:, g * w:(g + 1) * w]
        y_ref[0, :, g * w:(g + 1) * w] = yg

    @pl.when(c == pl.num_programs(1) - 1)
    def _():
        hl_ref[0] = st_ref[...]


def _pad_lanes(v, width=LANES):
    v = v.reshape(1, -1)
    return jnp.pad(v, ((0, 0), (0, width - v.shape[1])))


def _ssd(u3, h0, prefix8, conv_w, conv_b, dt_bias, a_log, d_skip, ssm_norm_w):
    b, t = u3.shape[:2]
    q = min(SSD_CHUNK, t)
    qp = max(q, 8)
    nc = t // q
    row = lambda shape: pl.BlockSpec(shape, lambda i, c: (0, 0))
    return pl.pallas_call(
        functools.partial(_ssd_kernel, q=q, qp=qp),
        grid=(b, nc),
        in_specs=[pl.BlockSpec((1, q, CONV_DIM), lambda i, c: (i, c, C_XBC // CONV_DIM)),
                  pl.BlockSpec((1, q, D_INNER), lambda i, c: (i, c, C_Z // D_INNER)),
                  pl.BlockSpec((1, q, LANES), lambda i, c: (i, c, C_SMALL // LANES)),
                  pl.BlockSpec((1, D_INNER, D_STATE), lambda i, c: (i, 0, 0)),
                  pl.BlockSpec((1, 8, CONV_DIM), lambda i, c: (i, 0, 0)),
                  row((CONV_W, CONV_DIM)), row((1, CONV_DIM)), row((1, LANES)), row((1, LANES)), row((1, LANES)),
                  row((1, D_INNER))],
        out_specs=[pl.BlockSpec((1, q, D_INNER), lambda i, c: (i, c, 0)),
                   pl.BlockSpec((1, D_INNER, D_STATE), lambda i, c: (i, 0, 0))],
        out_shape=[jax.ShapeDtypeStruct((b, t, D_INNER), F32),
                   jax.ShapeDtypeStruct((b, D_INNER, D_STATE), F32)],
        scratch_shapes=[pltpu.VMEM((D_INNER, D_STATE), F32), pltpu.VMEM((qp + 8, CONV_DIM), F32)],
        compiler_params=_cparams(("arbitrary", "arbitrary")),
        name="ssd",
    )(u3, u3, u3, h0, prefix8, conv_w, conv_b.reshape(1, CONV_DIM), _pad_lanes(dt_bias), _pad_lanes(a_log),
      _pad_lanes(d_skip), ssm_norm_w.reshape(1, D_INNER))


def _head_sumsq(x, bd):
    sq = x * x
    hi = sq.astype(BF16)
    lo = (sq - hi.astype(F32)).astype(BF16)
    return _mm(hi, bd) + _mm(lo, bd)


def _rope_tiles(x, cos, sin_signed):
    n_tiles = x.shape[1] // LANES
    lane = lax.broadcasted_iota(jnp.int32, (1, LANES), 1)
    first_half = (lane % HEAD_DIM) < (HEAD_DIM // 2)
    outs = []
    for i in range(n_tiles):
        xt = x[:, i * LANES:(i + 1) * LANES]
        swapped = jnp.where(first_half, pltpu.roll(xt, LANES - HEAD_DIM // 2, 1), pltpu.roll(xt, HEAD_DIM // 2, 1))
        outs.append(xt * cos + swapped * sin_signed)
    return jnp.concatenate(outs, axis=1)


def _nsa_prep_kernel(q_ref, kv_ref, cos_ref, sin_ref, bd_ref, qw_ref, kw_ref, qn_ref, qr_ref, slc_ref, win_ref):
    cos = cos_ref[...]
    sin = sin_ref[...]
    bd = bd_ref[...]
    q = q_ref[...]
    qn = q * lax.rsqrt(_head_sumsq(q, bd) * (1.0 / HEAD_DIM) + EPS) * qw_ref[...]
    qn_ref[...] = qn
    qr_ref[...] = _rope_tiles(qn, cos, sin)
    for br, o_ref in ((1, slc_ref), (2, win_ref)):
        k = kv_ref[:, br * ROW_W:br * ROW_W + KV_W]
        kn = k * lax.rsqrt(_head_sumsq(k, bd[:KV_W, :KV_W]) * (1.0 / HEAD_DIM) + EPS) * kw_ref[br:br + 1, :]
        o_ref[:, :KV_W] = _rope_tiles(kn, cos, sin)
        o_ref[:, KV_W:] = kv_ref[:, br * ROW_W + KV_W:(br + 1) * ROW_W]


def _nsa_prep(u2d, cos_t, sin_t, q_norm_w, k_norm_w, t):
    n = u2d.shape[0]
    tm = min(512, n, t) if t > 1 else n
    tab_blocks = max(t // tm, 1)
    if t == 1:
        cos_t = jnp.broadcast_to(cos_t, (tm, LANES))
        sin_t = jnp.broadcast_to(sin_t, (tm, LANES))
    head_id = np.arange(N_HEADS * HEAD_DIM) // HEAD_DIM
    bd = jnp.asarray(head_id[:, None] == head_id[None, :], BF16)
    qw = jnp.tile(q_norm_w, N_HEADS).reshape(1, -1)
    kw = jnp.tile(k_norm_w, (1, N_KV))
    tab = lambda: pl.BlockSpec((tm, LANES), lambda i: (i % tab_blocks, 0))
    return pl.pallas_call(
        _nsa_prep_kernel,
        grid=(n // tm,),
        in_specs=[pl.BlockSpec((tm, N_HEADS * HEAD_DIM), lambda i: (i, C_Q // (N_HEADS * HEAD_DIM))),
                  pl.BlockSpec((tm, 3 * ROW_W), lambda i: (i, C_KV // (3 * ROW_W))),
                  tab(), tab(),
                  pl.BlockSpec((N_HEADS * HEAD_DIM, N_HEADS * HEAD_DIM), lambda i: (0, 0)),
                  pl.BlockSpec((1, N_HEADS * HEAD_DIM), lambda i: (0, 0)),
                  pl.BlockSpec((3, KV_W), lambda i: (0, 0))],
        out_specs=[pl.BlockSpec((tm, N_HEADS * HEAD_DIM), lambda i: (i, 0)),
                   pl.BlockSpec((tm, N_HEADS * HEAD_DIM), lambda i: (i, 0)),
                   pl.BlockSpec((tm, ROW_W), lambda i: (i, 0)),
                   pl.BlockSpec((tm, ROW_W), lambda i: (i, 0))],
        out_shape=[jax.ShapeDtypeStruct((n, N_HEADS * HEAD_DIM), F32),
                   jax.ShapeDtypeStruct((n, N_HEADS * HEAD_DIM), F32),
                   jax.ShapeDtypeStruct((n, ROW_W), F32),
                   jax.ShapeDtypeStruct((n, ROW_W), F32)],
        compiler_params=_cparams(("arbitrary",)),
        name="nsa_prep",
    )(u2d, u2d, cos_t, sin_t, bd, qw, kw)


def _rope_tables(pos):
    half = HEAD_DIM // 2
    inv = ROPE_THETA ** (-jnp.arange(half, dtype=F32) / half)
    ang = pos.astype(F32)[:, None] * inv[None, :]
    cos = jnp.cos(ang)
    sin = jnp.sin(ang)
    reps = LANES // HEAD_DIM
    return (jnp.tile(jnp.concatenate([cos, cos], axis=1), (1, reps)),
            jnp.tile(jnp.concatenate([-sin, sin], axis=1), (1, reps)))


def _sub_block_proj(load_rows, wbd_ref, pos_ref):
    acc = [[None, None], [None, None]]
    per = KV_W // LANES
    for l in range(CMP_STRIDE):
        for cidx in range(2):
            rows = jnp.concatenate([load_rows(l, cidx * per + j) for j in range(per)], axis=1)
            for half in range(2):
                x = rows + pos_ref[half * CMP_STRIDE + l, cidx:cidx + 1, :]
                part = _mm(x.astype(BF16), wbd_ref[half * CMP_STRIDE + l, cidx])
                acc[half][cidx] = part if acc[half][cidx] is None else acc[half][cidx] + part
    return acc


def _compress_finish(first_k, first_v, second_k, second_v, kg):
    n_sub = first_k.shape[0]
    k = first_k + pltpu.roll(second_k, n_sub - 1, 0)
    v = first_v + pltpu.roll(second_v, n_sub - 1, 0)
    parts = []
    for hd in range(N_KV):
        kh = k[:, hd * HEAD_DIM:(hd + 1) * HEAD_DIM]
        parts.append(kh * lax.rsqrt(jnp.mean(kh * kh, axis=-1, keepdims=True) + EPS))
    return jnp.concatenate(parts, axis=1) * kg, v


def _compress_kernel(rows_ref, wbd_ref, pos_ref, kg_ref, kc_ref, vc_ref, *, n_sub):
    acc = _sub_block_proj(lambda l, j: rows_ref[0, pl.ds(l * ROW_TILES + j, n_sub, stride=CMP_STRIDE * ROW_TILES), :],
                          wbd_ref, pos_ref)
    kc, vc = _compress_finish(acc[0][0], acc[0][1], acc[1][0], acc[1][1], kg_ref[...])
    kc_ref[0] = kc
    vc_ref[0] = vc


def _compress_weights(w_cmp, cmp_pos, k_gain):
    eye = jnp.eye(N_KV, dtype=F32)
    wbd = jnp.einsum("hg,lcde->lchdge", eye, w_cmp).reshape(CMP_LEN, 2, KV_W, KV_W).astype(BF16)
    pos = jnp.tile(cmp_pos, (1, 1, N_KV))
    kg = jnp.tile(k_gain, N_KV).reshape(1, KV_W)
    return wbd, pos, kg


def _compress_prompt(cmp_rows, wbd, pos, kg):
    b, t = cmp_rows.shape[:2]
    cmp_rows = cmp_rows.reshape(b, t * ROW_TILES, LANES)
    n_sub = t // CMP_STRIDE
    const = lambda shape: pl.BlockSpec(shape, lambda i: (0,) * len(shape))
    return pl.pallas_call(
        functools.partial(_compress_kernel, n_sub=n_sub),
        grid=(b,),
        in_specs=[pl.BlockSpec((1, t * ROW_TILES, LANES), lambda i: (i, 0, 0)),
                  const((CMP_LEN, 2, KV_W, KV_W)), const((CMP_LEN, 2, KV_W)), const((1, KV_W))],
        out_specs=[pl.BlockSpec((1, n_sub, KV_W), lambda i: (i, 0, 0)),
                   pl.BlockSpec((1, n_sub, KV_W), lambda i: (i, 0, 0))],
        out_shape=[jax.ShapeDtypeStruct((b, n_sub, KV_W), F32), jax.ShapeDtypeStruct((b, n_sub, KV_W), F32)],
        compiler_params=_cparams(("arbitrary",)),
        name="compress_prompt",
    )(cmp_rows, wbd, pos, kg)


def _softmax_rows(s, mask):
    s = jnp.where(mask, s, NEG)
    m = jnp.max(s, axis=-1, keepdims=True)
    e = jnp.exp(s - m)
    p = e / jnp.sum(e, axis=-1, keepdims=True)
    return jnp.where(mask, p, 0.0)


def _stack_heads(x, kv):
    return jnp.concatenate([x[:, (kv * GQA + g) * HEAD_DIM:(kv * GQA + g + 1) * HEAD_DIM] for g in range(GQA)],
                           axis=0)


def _block_scores_t(p, tq, n_cmp, n_blk, sel_mat_t):
    imp = p[0:tq]
    for g in range(1, GQA):
        imp = imp + p[g * tq:(g + 1) * tq]
    return _nt(sel_mat_t, imp, precision=HI)


def _top_rank_mask_t(score_t, n_blk):
    jidx = lax.broadcasted_iota(jnp.int32, score_t.shape, 0)
    cnt = jnp.zeros(score_t.shape, F32)
    for j in range(n_blk):
        rowv = score_t[j:j + 1, :]
        beats = (rowv > score_t) | ((rowv == score_t) & (j < jidx))
        cnt = cnt + beats.astype(F32)
    return cnt < float(N_SEL)


def _nsa_prompt_kernel(qn_ref, qr_ref, sm_ref, kc_ref, vc_ref, slc_ref, win_ref, selm_ref, exp_ref, o_ref,
                       selk_ref, *, t, n_cmp, n_blk):
    tq = Q_BLOCK
    n = pl.program_id(1)
    start = n * tq
    qn = qn_ref[0]
    qr = qr_ref[0]
    gates = _sigmoid(sm_ref[0])
    q_pos1 = start + lax.broadcasted_iota(jnp.int32, (tq, 1), 0)
    q_pos = jnp.concatenate([q_pos1] * GQA, axis=0)
    ci = lax.broadcasted_iota(jnp.int32, (1, n_cmp), 1)
    c_mask = (ci * CMP_STRIDE + (CMP_LEN - 1)) <= q_pos
    jcol = lax.broadcasted_iota(jnp.int32, (n_blk, 1), 0)
    qrow = start + lax.broadcasted_iota(jnp.int32, (1, tq), 1)
    cur = qrow // SLC_BLOCK
    forced_t = (jcol == 0) | (jcol == cur) | (jcol == cur - 1)
    valid_t = jcol * SLC_BLOCK <= qrow
    w_lo = jnp.maximum(start - WINDOW, 0)
    w_len = WINDOW + tq
    kp_w = w_lo + lax.broadcasted_iota(jnp.int32, (1, w_len), 1)
    w_mask = (kp_w <= q_pos) & (kp_w > q_pos - WINDOW)
    tk = 2 * tq
    n_tiles = (n + 2) // 2
    kp_all = lax.broadcasted_iota(jnp.int32, (1, t), 1)
    eye_q = (lax.broadcasted_iota(jnp.int32, (tq, tq), 0)
             == lax.broadcasted_iota(jnp.int32, (tq, tq), 1)).astype(BF16)

    outs = []
    for kv in range(N_KV):
        qn_s = _stack_heads(qn, kv).astype(BF16)
        qr_s = (_stack_heads(qr, kv) * ATTN_SCALE).astype(BF16)
        lanes = slice(kv * HEAD_DIM, (kv + 1) * HEAD_DIM)
        vlanes = slice(KV_W + kv * HEAD_DIM, KV_W + (kv + 1) * HEAD_DIM)
        kc = kc_ref[0][:, lanes].astype(BF16)
        vc = vc_ref[0][:, lanes].astype(BF16)
        p = _softmax_rows(_nt(qn_s, kc) * ATTN_SCALE, c_mask)
        o_cmp = _mm(p.astype(BF16), vc)
        score_t = _block_scores_t(p, tq, n_cmp, n_blk, selm_ref[...])
        score_t = jnp.where(forced_t, score_t + FORCE_BONUS, score_t)
        score_t = jnp.where(valid_t, score_t, NEG)
        sel_t = _top_rank_mask_t(score_t, n_blk) & valid_t
        sel = _nt(eye_q, jnp.where(sel_t, 1.0, 0.0).astype(BF16))
        sel_keys = _mm(sel.astype(BF16), exp_ref[...])
        selk_ref[...] = jnp.where((sel_keys > 0.5) & (kp_all <= q_pos1), 0.0, NEG)

        def tile_body(i, carry, lanes=lanes, vlanes=vlanes, qr_s=qr_s):
            m, l, acc = carry
            k0 = pl.multiple_of(i * tk, tk)
            kt = slc_ref[0, pl.ds(k0, tk), lanes].astype(BF16)
            vt = slc_ref[0, pl.ds(k0, tk), vlanes].astype(BF16)
            bias = selk_ref[:, pl.ds(k0, tk)]
            s = _nt(qr_s, kt) + jnp.concatenate([bias] * GQA, axis=0)
            m_new = jnp.maximum(m, jnp.max(s, axis=-1, keepdims=True))
            alpha = jnp.exp(m - m_new)
            e = jnp.exp(s - m_new)
            return (m_new, alpha * l + jnp.sum(e, axis=-1, keepdims=True),
                    alpha * acc + _mm(e.astype(BF16), vt))

        m0 = jnp.full((GQA * tq, 1), NEG, F32)
        l0 = jnp.zeros((GQA * tq, 1), F32)
        a0 = jnp.zeros((GQA * tq, HEAD_DIM), F32)
        _, l_s, acc_s = lax.fori_loop(0, n_tiles, tile_body, (m0, l0, a0))
        o_slc = acc_s / l_s
        kw = win_ref[0, pl.ds(pl.multiple_of(w_lo, tq), w_len), lanes].astype(BF16)
        vw = win_ref[0, pl.ds(pl.multiple_of(w_lo, tq), w_len), vlanes].astype(BF16)
        pw = _softmax_rows(_nt(qr_s, kw), w_mask)
        o_win = _mm(pw.astype(BF16), vw)
        for g in range(GQA):
            hd = kv * GQA + g
            rows = slice(g * tq, (g + 1) * tq)
            g0 = gates[:, SSM_HEADS + hd:SSM_HEADS + hd + 1]
            g1 = gates[:, SSM_HEADS + N_HEADS + hd:SSM_HEADS + N_HEADS + hd + 1]
            g2 = gates[:, SSM_HEADS + 2 * N_HEADS + hd:SSM_HEADS + 2 * N_HEADS + hd + 1]
            outs.append(g0 * o_cmp[rows] + g1 * o_slc[rows] + g2 * o_win[rows])
    o_ref[0] = jnp.concatenate(outs, axis=1)


def _selection_matrices(n_cmp, n_blk, n_keys):
    i = np.arange(n_cmp)[None, :]
    j = np.arange(n_blk)[:, None]
    sel = ((i >= SUB_PER_SLC * j - 1) & (i <= SUB_PER_SLC * j + SUB_PER_SLC - 1)).astype(np.float32)
    key = np.arange(n_keys)[None, :]
    expand = (key // SLC_BLOCK == j).astype(np.float32)
    return jnp.asarray(sel), jnp.asarray(expand, BF16)


def _nsa_prompt(qn3, qr3, u3, kc, vc, slc3, win3):
    b, t = qn3.shape[:2]
    n_cmp = t // CMP_STRIDE
    n_blk = t // SLC_BLOCK
    selm, expand = _selection_matrices(n_cmp, n_blk, t)
    selm = selm.at[:, n_cmp - 1].set(0.0)
    hw = N_HEADS * HEAD_DIM
    full = lambda w: pl.BlockSpec((1, t, w), lambda i, n: (i, 0, 0))
    return pl.pallas_call(
        functools.partial(_nsa_prompt_kernel, t=t, n_cmp=n_cmp, n_blk=n_blk),
        grid=(b, t // Q_BLOCK),
        in_specs=[pl.BlockSpec((1, Q_BLOCK, hw), lambda i, n: (i, n, 0)),
                  pl.BlockSpec((1, Q_BLOCK, hw), lambda i, n: (i, n, 0)),
                  pl.BlockSpec((1, Q_BLOCK, LANES), lambda i, n: (i, n, C_SMALL // LANES)),
                  pl.BlockSpec((1, n_cmp, KV_W), lambda i, n: (i, 0, 0)),
                  pl.BlockSpec((1, n_cmp, KV_W), lambda i, n: (i, 0, 0)),
                  full(ROW_W), full(ROW_W),
                  pl.BlockSpec((n_blk, n_cmp), lambda i, n: (0, 0)),
                  pl.BlockSpec((n_blk, t), lambda i, n: (0, 0))],
        out_specs=pl.BlockSpec((1, Q_BLOCK, hw), lambda i, n: (i, n, 0)),
        out_shape=jax.ShapeDtypeStruct((b, t, hw), F32),
        scratch_shapes=[pltpu.VMEM((Q_BLOCK, t), F32)],
        compiler_params=_cparams(("arbitrary", "arbitrary")),
        name="nsa_prompt",
    )(qn3, qr3, u3, kc, vc, slc3, win3, selm, expand)


N_PARTS = 2


def _nsa_sample_kernel(pt_ref, qn_ref, qr_ref, sm_ref, slcn_ref, winn_ref, wins_ref, wbd_ref, pos_ref, kg_ref,
                       selm_ref, cmp_hbm, slc_hbm, o_ref,
                       buf_ref, sem_ref, fs_ref, sel_ref, ocmp_ref, m_ref, l_ref, acc_ref,
                       *, past, ppp, n_blk_pad, batch):
    b = pl.program_id(0)
    s = pl.program_id(1)
    n_steps = 2 * N_PARTS
    c = b * n_steps + s
    total = batch * n_steps
    rp = ppp * PAGE_SIZE
    nsp = rp // CMP_STRIDE
    n_sub = N_PARTS * nsp
    win_buf = wins_ref.shape[1]
    cur = past // SLC_BLOCK

    def page_copy(hbm, bb, part, pg, slot):
        tiles = PAGE_SIZE * ROW_TILES
        return pltpu.make_async_copy(hbm.at[pt_ref[bb, part * ppp + pg]],
                                     buf_ref.at[slot, pl.ds(pg * tiles, tiles)], sem_ref.at[slot])

    def load_rows(first, count, stride, tile):
        return buf_ref[slot, pl.ds(first * ROW_TILES + tile, count, stride=stride * ROW_TILES), :]

    def start_chunk(cc, slot):
        bb = cc // n_steps
        ss = cc % n_steps

        @pl.when(ss < N_PARTS)
        def _():
            for pg in range(ppp):
                page_copy(cmp_hbm, bb, ss, pg, slot).start()

        @pl.when(ss >= N_PARTS)
        def _():
            for pg in range(ppp):
                page_copy(slc_hbm, bb, ss - N_PARTS, pg, slot).start()

    @pl.when(c == 0)
    def _():
        start_chunk(c, 0)

    @pl.when(c + 1 < total)
    def _():
        start_chunk(c + 1, (c + 1) % 2)

    slot = c % 2
    for pg in range(ppp):
        page_copy(cmp_hbm, b, 0, pg, slot).wait()

    def stack_q(ref, kv, scale):
        rows = [ref[0, :, (kv * GQA + g) * HEAD_DIM:(kv * GQA + g + 1) * HEAD_DIM] for g in range(GQA)]
        return jnp.concatenate(rows + [jnp.zeros((8 - GQA, HEAD_DIM), F32)], axis=0) * scale

    @pl.when(s < N_PARTS)
    def _():
        acc = _sub_block_proj(lambda l, j: load_rows(l, nsp, CMP_STRIDE, j), wbd_ref, pos_ref)
        row0 = pl.multiple_of(s * nsp, nsp)
        fs_ref[0, pl.ds(row0, nsp), :] = acc[0][0]
        fs_ref[1, pl.ds(row0, nsp), :] = acc[0][1]
        fs_ref[2, pl.ds(row0, nsp), :] = acc[1][0]
        fs_ref[3, pl.ds(row0, nsp), :] = acc[1][1]

    @pl.when(s == N_PARTS - 1)
    def _():
        kc, vc = _compress_finish(fs_ref[0], fs_ref[1], fs_ref[2], fs_ref[3], kg_ref[...])
        ci = lax.broadcasted_iota(jnp.int32, (1, n_sub), 1)
        c_mask = (ci * CMP_STRIDE + (CMP_LEN - 1)) <= past
        imps = []
        for kv in range(N_KV):
            lanes = slice(kv * HEAD_DIM, (kv + 1) * HEAD_DIM)
            q = stack_q(qn_ref, kv, 1.0).astype(BF16)
            p = _softmax_rows(_nt(q, kc[:, lanes].astype(BF16)) * ATTN_SCALE, c_mask)
            ocmp_ref[kv] = _mm(p.astype(BF16), vc[:, lanes].astype(BF16))
            imps.append(p[0:1] + p[1:2] + p[2:3] + p[3:4])
        imp = jnp.concatenate(imps + [jnp.zeros((8 - N_KV, n_sub), F32)], axis=0)
        score = _mm(imp, selm_ref[...], precision=HI)
        jrow = lax.broadcasted_iota(jnp.int32, (1, n_blk_pad), 1)
        forced = (jrow == 0) | (jrow == cur) | (jrow == cur - 1)
        valid = jrow * SLC_BLOCK <= past
        score = jnp.where(forced, score + FORCE_BONUS, score)
        score = jnp.where(valid, score, NEG)
        eye = (lax.broadcasted_iota(jnp.int32, (n_blk_pad, n_blk_pad), 0)
               == lax.broadcasted_iota(jnp.int32, (n_blk_pad, n_blk_pad), 1)).astype(F32)
        score_c = _nt(eye, score, precision=HI)
        jc = lax.broadcasted_iota(jnp.int32, (n_blk_pad, n_blk_pad), 0)
        jr = lax.broadcasted_iota(jnp.int32, (n_blk_pad, n_blk_pad), 1)
        sels = []
        for kv in range(N_KV):
            colv = score_c[:, kv:kv + 1]
            rowv = score[kv:kv + 1, :]
            beats = (colv > rowv) | ((colv == rowv) & (jc < jr))
            rank = jnp.sum(beats.astype(F32), axis=0, keepdims=True)
            sels.append(jnp.where((rank < float(N_SEL)) & valid, 1.0, 0.0))
        sel_ref[...] = jnp.concatenate(sels + [jnp.zeros((8 - N_KV, n_blk_pad), F32)], axis=0)

    @pl.when(s >= N_PARTS)
    def _():
        part = s - N_PARTS

        @pl.when(part == 0)
        def _():
            m_ref[...] = jnp.full(m_ref.shape, NEG, F32)
            l_ref[...] = jnp.zeros(l_ref.shape, F32)
            acc_ref[...] = jnp.zeros(acc_ref.shape, F32)

        key = part * rp + lax.broadcasted_iota(jnp.int32, (n_blk_pad, rp), 1)
        blk = lax.broadcasted_iota(jnp.int32, (n_blk_pad, rp), 0)
        expand = jnp.where(key // SLC_BLOCK == blk, 1.0, 0.0).astype(BF16)
        selk = _mm(sel_ref[...].astype(BF16), expand)
        for kv in range(N_KV):
            q = stack_q(qr_ref, kv, ATTN_SCALE).astype(BF16)
            off = (kv * HEAD_DIM) % LANES
            kt = load_rows(0, rp, 1, (kv * HEAD_DIM) // LANES)[:, off:off + HEAD_DIM].astype(BF16)
            vt = load_rows(0, rp, 1, (KV_W + kv * HEAD_DIM) // LANES)[:, off:off + HEAD_DIM].astype(BF16)
            ok = selk[kv:kv + 1, :] > 0.5
            sc = jnp.where(ok, _nt(q, kt), NEG)
            m_old = m_ref[kv]
            m_new = jnp.maximum(m_old, jnp.max(sc, axis=-1, keepdims=True))
            alpha = jnp.exp(m_old - m_new)
            e = jnp.where(ok, jnp.exp(sc - m_new), 0.0)
            m_ref[kv] = m_new
            l_ref[kv] = alpha * l_ref[kv] + jnp.sum(e, axis=-1, keepdims=True)
            acc_ref[kv] = alpha * acc_ref[kv] + _mm(e.astype(BF16), vt)

    @pl.when(s == n_steps - 1)
    def _():
        gates = _sigmoid(sm_ref[0])
        kpos = past - win_buf + lax.broadcasted_iota(jnp.int32, (1, win_buf), 1)
        w_mask = (kpos <= past) & (kpos > past - WINDOW) & (kpos >= 0)
        outs = []
        for kv in range(N_KV):
            lanes = slice(kv * HEAD_DIM, (kv + 1) * HEAD_DIM)
            vlanes = slice(KV_W + kv * HEAD_DIM, KV_W + (kv + 1) * HEAD_DIM)
            q = stack_q(qr_ref, kv, ATTN_SCALE)
            qb = q.astype(BF16)
            k_new = slcn_ref[0, :, lanes].astype(BF16).astype(F32)
            v_new = slcn_ref[0, :, vlanes].astype(BF16).astype(F32)
            ok_new = sel_ref[kv:kv + 1, cur:cur + 1] > 0.5
            s_new = jnp.where(ok_new, jnp.sum(qb.astype(F32) * k_new, axis=-1, keepdims=True), NEG)
            m_old = m_ref[kv]
            m_new = jnp.maximum(m_old, s_new)
            alpha = jnp.exp(m_old - m_new)
            e_new = jnp.where(ok_new, jnp.exp(s_new - m_new), 0.0)
            l_s = alpha * l_ref[kv] + e_new
            o_slc = (alpha * acc_ref[kv] + e_new.astype(BF16).astype(F32) * v_new) / l_s
            kw = wins_ref[0, :, lanes].astype(BF16)
            vw = wins_ref[0, :, vlanes].astype(BF16)
            sw = jnp.where(w_mask, _nt(qb, kw), NEG)
            kw_new = winn_ref[0, :, lanes].astype(BF16).astype(F32)
            vw_new = winn_ref[0, :, vlanes].astype(BF16).astype(F32)
            sw_new = jnp.sum(qb.astype(F32) * kw_new, axis=-1, keepdims=True)
            mw = jnp.maximum(jnp.max(sw, axis=-1, keepdims=True), sw_new)
            ew = jnp.where(w_mask, jnp.exp(sw - mw), 0.0)
            ew_new = jnp.exp(sw_new - mw)
            lw = jnp.sum(ew, axis=-1, keepdims=True) + ew_new
            pw = ew / lw
            o_win = _mm(pw.astype(BF16), vw) + (ew_new / lw).astype(BF16).astype(F32) * vw_new
            o_cmp = ocmp_ref[kv]
            for g in range(GQA):
                hd = kv * GQA + g
                g0 = gates[:, SSM_HEADS + hd:SSM_HEADS + hd + 1]
                g1 = gates[:, SSM_HEADS + N_HEADS + hd:SSM_HEADS + N_HEADS + hd + 1]
                g2 = gates[:, SSM_HEADS + 2 * N_HEADS + hd:SSM_HEADS + 2 * N_HEADS + hd + 1]
                outs.append(g0 * o_cmp[g:g + 1] + g1 * o_slc[g:g + 1] + g2 * o_win[g:g + 1])
        o_ref[0] = jnp.concatenate(outs, axis=1)


def _nsa_sample(qn, qr, u2d, slc_new, win_new, win_state, page_table, cache_cmp, cache_slc, wbd, pos, kg):
    b = qn.shape[0]
    n_pages = page_table.shape[1]
    past = n_pages * PAGE_SIZE
    ppp = n_pages // N_PARTS
    n_sub = past // CMP_STRIDE
    n_blk = past // SLC_BLOCK + 1
    n_blk_pad = -(-n_blk // LANES) * LANES
    selm, _ = _selection_matrices(n_sub, n_blk_pad, 1)
    selm = selm.at[:, n_sub - 1].set(0.0).T
    hw = N_HEADS * HEAD_DIM
    win_buf = win_state.shape[1]
    n_steps = 2 * N_PARTS
    per_b = lambda w: pl.BlockSpec((1, 1, w), lambda i, s, pt: (i, 0, 0))
    const = lambda shape: pl.BlockSpec(shape, lambda i, s, pt: (0,) * len(shape))
    grid_spec = pltpu.PrefetchScalarGridSpec(
        num_scalar_prefetch=1,
        grid=(b, n_steps),
        in_specs=[per_b(hw), per_b(hw),
                  pl.BlockSpec((1, 1, LANES), lambda i, s, pt: (i, 0, C_SMALL // LANES)),
                  per_b(ROW_W), per_b(ROW_W),
                  pl.BlockSpec((1, win_buf, ROW_W), lambda i, s, pt: (i, 0, 0)),
                  const((CMP_LEN, 2, KV_W, KV_W)), const((CMP_LEN, 2, KV_W)), const((1, KV_W)),
                  const((n_sub, n_blk_pad)),
                  pl.BlockSpec(memory_space=pl.ANY), pl.BlockSpec(memory_space=pl.ANY)],
        out_specs=per_b(hw),
        scratch_shapes=[pltpu.VMEM((2, ppp * PAGE_SIZE * ROW_TILES, LANES), F32), pltpu.SemaphoreType.DMA((2,)),
                        pltpu.VMEM((4, n_sub, KV_W), F32), pltpu.VMEM((8, n_blk_pad), F32),
                        pltpu.VMEM((N_KV, 8, HEAD_DIM), F32), pltpu.VMEM((N_KV, 8, 1), F32),
                        pltpu.VMEM((N_KV, 8, 1), F32), pltpu.VMEM((N_KV, 8, HEAD_DIM), F32)])
    return pl.pallas_call(
        functools.partial(_nsa_sample_kernel, past=past, ppp=ppp, n_blk_pad=n_blk_pad, batch=b),
        grid_spec=grid_spec,
        out_shape=jax.ShapeDtypeStruct((b, 1, hw), F32),
        compiler_params=_cparams(("arbitrary", "arbitrary")),
        name="nsa_sample",
    )(page_table, qn.reshape(b, 1, hw), qr.reshape(b, 1, hw), u2d.reshape(b, 1, D_IN_P),
      slc_new.reshape(b, 1, ROW_W), win_new.reshape(b, 1, ROW_W), win_state, wbd, pos, kg, selm,
      cache_cmp, cache_slc).reshape(b, hw)


def _head_block_diag(x_ref, scale):
    q16 = jnp.concatenate([x_ref[0, :, hd * HEAD_DIM:(hd + 1) * HEAD_DIM] for hd in range(N_HEADS)], axis=0) * scale
    kv_of_row = lax.broadcasted_iota(jnp.int32, (N_HEADS, HEAD_DIM), 0) // GQA
    return jnp.concatenate([jnp.where(kv_of_row == kv, q16, 0.0) for kv in range(N_KV)], axis=1)


def _nsa_sample_t_kernel(pt_ref, qn_ref, qr_ref, sm_ref, slcn_ref, winn_ref, wins_ref, wbd_ref, pos_ref, kg_ref,
                         selm_ref, cmp_hbm, slc_hbm, o_ref,
                         buf_ref, sem_ref, rows_ref, fs_ref, sel_ref, ocmp_ref, m_ref, l_ref, acc_ref,
                         *, past, ppp, n_blk_pad, batch):
    b = pl.program_id(0)
    s = pl.program_id(1)
    n_steps = 2 * N_PARTS
    c = b * n_steps + s
    total = batch * n_steps
    rp = ppp * PAGE_SIZE
    nsp = rp // CMP_STRIDE
    n_sub = N_PARTS * nsp
    win_buf = wins_ref.shape[1]
    cur = past // SLC_BLOCK

    def page_copy(hbm, bb, part, pg, slot):
        return pltpu.make_async_copy(hbm.at[pt_ref[bb, part * ppp + pg]],
                                     buf_ref.at[slot, pl.ds(pg * ROW_W, ROW_W)], sem_ref.at[slot])

    def start_chunk(cc, slot):
        bb = cc // n_steps
        ss = cc % n_steps

        @pl.when(ss < N_PARTS)
        def _():
            for pg in range(ppp):
                page_copy(cmp_hbm, bb, ss, pg, slot).start()

        @pl.when(ss >= N_PARTS)
        def _():
            for pg in range(ppp):
                page_copy(slc_hbm, bb, ss - N_PARTS, pg, slot).start()

    @pl.when(c == 0)
    def _():
        start_chunk(c, 0)

    @pl.when(c + 1 < total)
    def _():
        start_chunk(c + 1, (c + 1) % 2)

    slot = c % 2
    for pg in range(ppp):
        page_copy(cmp_hbm, b, 0, pg, slot).wait()

    @pl.when(s < N_PARTS)
    def _():
        for pg in range(ppp):
            for j in range(ROW_TILES):
                tile = buf_ref[slot, pg * ROW_W + j * LANES:pg * ROW_W + (j + 1) * LANES, :]
                rows_ref[j, pg * PAGE_SIZE:(pg + 1) * PAGE_SIZE, :] = tile.T
        acc = _sub_block_proj(lambda l, j: rows_ref[j, pl.ds(l, nsp, stride=CMP_STRIDE), :], wbd_ref, pos_ref)
        row0 = pl.multiple_of(s * nsp, nsp)
        fs_ref[0, pl.ds(row0, nsp), :] = acc[0][0]
        fs_ref[1, pl.ds(row0, nsp), :] = acc[0][1]
        fs_ref[2, pl.ds(row0, nsp), :] = acc[1][0]
        fs_ref[3, pl.ds(row0, nsp), :] = acc[1][1]

    @pl.when(s == N_PARTS - 1)
    def _():
        kc, vc = _compress_finish(fs_ref[0], fs_ref[1], fs_ref[2], fs_ref[3], kg_ref[...])
        ci = lax.broadcasted_iota(jnp.int32, (1, n_sub), 1)
        c_mask = (ci * CMP_STRIDE + (CMP_LEN - 1)) <= past
        qn_bd = _head_block_diag(qn_ref, 1.0).astype(BF16)
        p = _softmax_rows(_nt(qn_bd, kc.astype(BF16)) * ATTN_SCALE, c_mask)
        ocmp_ref[...] = _mm(p.astype(BF16), vc.astype(BF16))
        imps = [p[kv * GQA:kv * GQA + 1] + p[kv * GQA + 1:kv * GQA + 2] + p[kv * GQA + 2:kv * GQA + 3]
                + p[kv * GQA + 3:kv * GQA + 4] for kv in range(N_KV)]
        imp = jnp.concatenate(imps + [jnp.zeros((8 - N_KV, n_sub), F32)], axis=0)
        score = _mm(imp, selm_ref[...], precision=HI)
        jrow = lax.broadcasted_iota(jnp.int32, (1, n_blk_pad), 1)
        forced = (jrow == 0) | (jrow == cur) | (jrow == cur - 1)
        valid = jrow * SLC_BLOCK <= past
        score = jnp.where(forced, score + FORCE_BONUS, score)
        score = jnp.where(valid, score, NEG)
        eye = (lax.broadcasted_iota(jnp.int32, (n_blk_pad, n_blk_pad), 0)
               == lax.broadcasted_iota(jnp.int32, (n_blk_pad, n_blk_pad), 1)).astype(F32)
        score_c = _nt(eye, score, precision=HI)
        jc = lax.broadcasted_iota(jnp.int32, (n_blk_pad, n_blk_pad), 0)
        jr = lax.broadcasted_iota(jnp.int32, (n_blk_pad, n_blk_pad), 1)
        sels = []
        for kv in range(N_KV):
            colv = score_c[:, kv:kv + 1]
            rowv = score[kv:kv + 1, :]
            beats = (colv > rowv) | ((colv == rowv) & (jc < jr))
            rank = jnp.sum(beats.astype(F32), axis=0, keepdims=True)
            sel = jnp.where((rank < float(N_SEL)) & valid, 1.0, 0.0)
            sels.extend([sel] * GQA)
        sel_ref[...] = jnp.concatenate(sels, axis=0)

    @pl.when(s >= N_PARTS)
    def _():
        part = s - N_PARTS

        @pl.when(part == 0)
        def _():
            m_ref[...] = jnp.full(m_ref.shape, NEG, F32)
            l_ref[...] = jnp.zeros(l_ref.shape, F32)
            acc_ref[...] = jnp.zeros(acc_ref.shape, F32)

        key = part * rp + lax.broadcasted_iota(jnp.int32, (n_blk_pad, rp), 1)
        blk = lax.broadcasted_iota(jnp.int32, (n_blk_pad, rp), 0)
        expand = jnp.where(key // SLC_BLOCK == blk, 1.0, 0.0).astype(BF16)
        ok = _mm(sel_ref[...].astype(BF16), expand) > 0.5
        qr_bd = _head_block_diag(qr_ref, ATTN_SCALE).astype(BF16)
        sc = jnp.concatenate([_mm(qr_bd, buf_ref[slot, pg * ROW_W:pg * ROW_W + KV_W, :].astype(BF16))
                              for pg in range(ppp)], axis=1)
        sc = jnp.where(ok, sc, NEG)
        m_old = m_ref[...]
        m_new = jnp.maximum(m_old, jnp.max(sc, axis=-1, keepdims=True))
        alpha = jnp.exp(m_old - m_new)
        e = jnp.where(ok, jnp.exp(sc - m_new), 0.0)
        m_ref[...] = m_new
        l_ref[...] = alpha * l_ref[...] + jnp.sum(e, axis=-1, keepdims=True)
        eb = e.astype(BF16)
        pv = None
        for pg in range(ppp):
            part_pv = _nt(eb[:, pg * PAGE_SIZE:(pg + 1) * PAGE_SIZE],
                          buf_ref[slot, pg * ROW_W + KV_W:(pg + 1) * ROW_W, :].astype(BF16))
            pv = part_pv if pv is None else pv + part_pv
        acc_ref[...] = alpha * acc_ref[...] + pv

    @pl.when(s == n_steps - 1)
    def _():
        qr_bd = _head_block_diag(qr_ref, ATTN_SCALE).astype(BF16).astype(F32)
        k_new = slcn_ref[0, :, :KV_W].astype(BF16).astype(F32)
        v_new = slcn_ref[0, :, KV_W:].astype(BF16).astype(F32)
        ok_new = sel_ref[:, cur:cur + 1] > 0.5
        s_new = jnp.where(ok_new, jnp.sum(qr_bd * k_new, axis=-1, keepdims=True), NEG)
        m_old = m_ref[...]
        m_new = jnp.maximum(m_old, s_new)
        alpha = jnp.exp(m_old - m_new)
        e_new = jnp.where(ok_new, jnp.exp(s_new - m_new), 0.0)
        l_s = alpha * l_ref[...] + e_new
        o_slc = (alpha * acc_ref[...] + e_new.astype(BF16).astype(F32) * v_new) / l_s
        kpos = past - win_buf + lax.broadcasted_iota(jnp.int32, (1, win_buf), 1)
        w_mask = (kpos <= past) & (kpos > past - WINDOW) & (kpos >= 0)
        sw = jnp.where(w_mask, _nt(qr_bd.astype(BF16), wins_ref[0, :, :KV_W].astype(BF16)), NEG)
        kw_new = winn_ref[0, :, :KV_W].astype(BF16).astype(F32)
        vw_new = winn_ref[0, :, KV_W:].astype(BF16).astype(F32)
        sw_new = jnp.sum(qr_bd * kw_new, axis=-1, keepdims=True)
        mw = jnp.maximum(jnp.max(sw, axis=-1, keepdims=True), sw_new)
        ew = jnp.where(w_mask, jnp.exp(sw - mw), 0.0)
        ew_new = jnp.exp(sw_new - mw)
        lw = jnp.sum(ew, axis=-1, keepdims=True) + ew_new
        o_win = (_mm((ew / lw).astype(BF16), wins_ref[0, :, KV_W:].astype(BF16))
                 + (ew_new / lw).astype(BF16).astype(F32) * vw_new)
        gates = _sigmoid(sm_ref[0])
        hrow = lax.broadcasted_iota(jnp.int32, (N_HEADS, LANES), 0)
        hlane = lax.broadcasted_iota(jnp.int32, (N_HEADS, LANES), 1)
        gcol = [jnp.sum(jnp.where(hlane == SSM_HEADS + br * N_HEADS + hrow, gates, 0.0), axis=-1, keepdims=True)
                for br in range(3)]
        mixed = gcol[0] * ocmp_ref[...] + gcol[1] * o_slc + gcol[2] * o_win
        o_ref[0] = jnp.concatenate(
            [mixed[hd:hd + 1, (hd // GQA) * HEAD_DIM:(hd // GQA + 1) * HEAD_DIM] for hd in range(N_HEADS)], axis=1)


def _nsa_sample_t(qn, qr, u2d, slc_new, win_new, win_state, page_table, cache_cmp_t, cache_slc_t, wbd, pos, kg):
    b = qn.shape[0]
    n_pages = page_table.shape[1]
    past = n_pages * PAGE_SIZE
    ppp = n_pages // N_PARTS
    n_sub = past // CMP_STRIDE
    n_blk = past // SLC_BLOCK + 1
    n_blk_pad = -(-n_blk // LANES) * LANES
    selm, _ = _selection_matrices(n_sub, n_blk_pad, 1)
    selm = selm.at[:, n_sub - 1].set(0.0).T
    hw = N_HEADS * HEAD_DIM
    win_buf = win_state.shape[1]
    n_steps = 2 * N_PARTS
    rp = ppp * PAGE_SIZE
    per_b = lambda w: pl.BlockSpec((1, 1, w), lambda i, s, pt: (i, 0, 0))
    const = lambda shape: pl.BlockSpec(shape, lambda i, s, pt: (0,) * len(shape))
    grid_spec = pltpu.PrefetchScalarGridSpec(
        num_scalar_prefetch=1,
        grid=(b, n_steps),
        in_specs=[per_b(hw), per_b(hw),
                  pl.BlockSpec((1, 1, LANES), lambda i, s, pt: (i, 0, C_SMALL // LANES)),
                  per_b(ROW_W), per_b(ROW_W),
                  pl.BlockSpec((1, win_buf, ROW_W), lambda i, s, pt: (i, 0, 0)),
                  const((CMP_LEN, 2, KV_W, KV_W)), const((CMP_LEN, 2, KV_W)), const((1, KV_W)),
                  const((n_sub, n_blk_pad)),
                  pl.BlockSpec(memory_space=pl.ANY), pl.BlockSpec(memory_space=pl.ANY)],
        out_specs=per_b(hw),
        scratch_shapes=[pltpu.VMEM((2, ppp * ROW_W, LANES), F32), pltpu.SemaphoreType.DMA((2,)),
                        pltpu.VMEM((ROW_TILES, rp, LANES), F32),
                        pltpu.VMEM((4, n_sub, KV_W), F32), pltpu.VMEM((N_HEADS, n_blk_pad), F32),
                        pltpu.VMEM((N_HEADS, KV_W), F32), pltpu.VMEM((N_HEADS, 1), F32),
                        pltpu.VMEM((N_HEADS, 1), F32), pltpu.VMEM((N_HEADS, KV_W), F32)])
    return pl.pallas_call(
        functools.partial(_nsa_sample_t_kernel, past=past, ppp=ppp, n_blk_pad=n_blk_pad, batch=b),
        grid_spec=grid_spec,
        out_shape=jax.ShapeDtypeStruct((b, 1, hw), F32),
        compiler_params=_cparams(("arbitrary", "arbitrary")),
        name="nsa_sample",
    )(page_table, qn.reshape(b, 1, hw), qr.reshape(b, 1, hw), u2d.reshape(b, 1, D_IN_P),
      slc_new.reshape(b, 1, ROW_W), win_new.reshape(b, 1, ROW_W), win_state, wbd, pos, kg, selm,
      cache_cmp_t, cache_slc_t).reshape(b, hw)


def _merge_kernel(x_ref, y_ref, o_ref, mg_ref, wps_ref, wpa_ref, wo_ref, n2_ref, wpq_ref, x2_ref, h2_ref, pq_ref):
    mg = _sigmoid(mg_ref[...])
    mixed = (mg[:, :D_MODEL] * _mm(y_ref[...].astype(BF16), wps_ref[...])
             + mg[:, D_MODEL:] * _mm(o_ref[...].astype(BF16), wpa_ref[...]))
    x2 = x_ref[...] + _mm(mixed.astype(BF16), wo_ref[...])
    x2_ref[...] = x2
    h2 = x2 * lax.rsqrt(jnp.mean(x2 * x2, axis=-1, keepdims=True) + EPS) * n2_ref[...]
    h2_ref[...] = h2
    pq_ref[...] = _mm(h2.astype(BF16), wpq_ref[...])


def _merge(x2d, y_ssm, o_attn, u2d, wps, wpa, wo, norm2_w, wpq):
    n = x2d.shape[0]
    tm = min(256, n)
    rows = lambda w, cb=0: pl.BlockSpec((tm, w), lambda i: (i, cb))
    const = lambda shape: pl.BlockSpec(shape, lambda i: (0, 0))
    out = jax.ShapeDtypeStruct((n, D_MODEL), F32)
    return pl.pallas_call(
        _merge_kernel,
        grid=(n // tm,),
        in_specs=[rows(D_MODEL), rows(D_INNER), rows(N_HEADS * HEAD_DIM), rows(2 * D_MODEL, C_MERGE // (2 * D_MODEL)),
                  const((D_INNER, D_MODEL)), const((N_HEADS * HEAD_DIM, D_MODEL)), const((D_MODEL, D_MODEL)),
                  const((1, D_MODEL)), const((D_MODEL, PEER_HEADS * PEER_DK))],
        out_specs=[rows(D_MODEL), rows(D_MODEL), rows(PEER_HEADS * PEER_DK)],
        out_shape=[out, out, jax.ShapeDtypeStruct((n, PEER_HEADS * PEER_DK), F32)],
        compiler_params=_cparams(("arbitrary",)),
        name="merge",
    )(x2d, y_ssm, o_attn, u2d, wps, wpa, wo, norm2_w.reshape(1, D_MODEL), wpq)


def _top_k_rows(s, k):
    r = s.shape[0]
    ridx = lax.broadcasted_iota(jnp.int32, s.shape, 0)
    vals, idxs = [], []
    for _ in range(k):
        m = jnp.max(s, axis=0, keepdims=True)
        idx = jnp.min(jnp.where(s == m, ridx, r), axis=0, keepdims=True)
        vals.append(m)
        idxs.append(idx)
        s = jnp.where(ridx == idx, -jnp.inf, s)
    return jnp.concatenate(vals, axis=0), jnp.concatenate(idxs, axis=0)


def _peer_select_kernel(pq_ref, sk_ref, e_ref, g_ref):
    half = PEER_DK // 2
    es, gs = [], []
    for hd in range(PEER_HEADS):
        tops = []
        for cidx in range(2):
            col = (hd * 2 + cidx) * half
            q = pq_ref[:, col:col + half].astype(BF16)
            tops.append(_top_k_rows(_nt(sk_ref[hd, cidx], q), PEER_TOPK))
        (s1, i1), (s2, i2) = tops
        cand = jnp.concatenate([s1[a:a + 1] + s2 for a in range(PEER_TOPK)], axis=0)
        top_s, top_c = _top_k_rows(cand, PEER_TOPK)
        a_code = top_c // PEER_TOPK
        b_code = top_c % PEER_TOPK
        i1s = jnp.zeros_like(top_c)
        i2s = jnp.zeros_like(top_c)
        for a in range(PEER_TOPK):
            i1s = jnp.where(a_code == a, i1[a:a + 1], i1s)
            i2s = jnp.where(b_code == a, i2[a:a + 1], i2s)
        es.append(i1s * N_KEYS + i2s)
        ex = jnp.exp(top_s - top_s[0:1])
        gs.append(ex / jnp.sum(ex, axis=0, keepdims=True))
    e_ref[...] = jnp.concatenate(es, axis=0).T
    g_ref[...] = jnp.concatenate(gs, axis=0).T


def _peer_select(pq, sub_keys_bf):
    n = pq.shape[0]
    tn = min(256, n)
    nk = PEER_HEADS * PEER_TOPK
    return pl.pallas_call(
        _peer_select_kernel,
        grid=(n // tn,),
        in_specs=[pl.BlockSpec((tn, PEER_HEADS * PEER_DK), lambda i: (i, 0)),
                  pl.BlockSpec((PEER_HEADS, 2, N_KEYS, PEER_DK // 2), lambda i: (0, 0, 0, 0))],
        out_specs=[pl.BlockSpec((tn, nk), lambda i: (i, 0)), pl.BlockSpec((tn, nk), lambda i: (i, 0))],
        out_shape=[jax.ShapeDtypeStruct((n, nk), jnp.int32), jax.ShapeDtypeStruct((n, nk), F32)],
        compiler_params=_cparams(("arbitrary",)),
        name="peer_select",
    )(pq, sub_keys_bf)


PEER_TOK = 8


def _peer_gather_kernel(e_ref, en_ref, g_ref, h_ref, x_ref, uv_hbm, o_ref, buf_ref, sem_ref, *, n):
    i = pl.program_id(0)
    nk = PEER_HEADS * PEER_TOPK
    rows = PEER_TOK * nk

    def issue(idx_ref, slot):
        def tok_body(t, carry):
            for k in range(nk):
                pltpu.make_async_copy(uv_hbm.at[pl.ds(idx_ref[t, k], 1)],
                                      buf_ref.at[slot, pl.ds(t * nk + k, 1)], sem_ref.at[slot]).start()
            return carry
        lax.fori_loop(0, PEER_TOK, tok_body, 0)

    @pl.when(i == 0)
    def _():
        issue(e_ref, 0)

    @pl.when(i + 1 < n)
    def _():
        issue(en_ref, (i + 1) % 2)

    slot = i % 2
    pltpu.make_async_copy(uv_hbm.at[pl.ds(0, rows)], buf_ref.at[slot], sem_ref.at[slot]).wait()

    eye = (lax.broadcasted_iota(jnp.int32, (nk, nk), 0) == lax.broadcasted_iota(jnp.int32, (nk, nk), 1)).astype(F32)
    g_t = _nt(eye, g_ref[...], precision=HI)
    outs = []
    for t in range(PEER_TOK):
        w = buf_ref[slot, t * nk:(t + 1) * nk, :]
        u = lax.bitcast_convert_type(w << 16, F32)
        v = lax.bitcast_convert_type(w & jnp.uint32(0xFFFF0000), F32)
        pre = jnp.sum(u * h_ref[t:t + 1, :], axis=-1, keepdims=True)
        act = 0.5 * pre * (1.0 + lax.erf(pre * np.float32(math.sqrt(0.5))))
        outs.append(jnp.sum((g_t[:, t:t + 1] * act) * v, axis=0, keepdims=True))
    o_ref[...] = x_ref[...] + jnp.concatenate(outs, axis=0)


def _peer_gather(e, g, h2, x2, uv):
    n = h2.shape[0]
    nk = PEER_HEADS * PEER_TOPK
    steps = n // PEER_TOK
    blk = lambda w: pl.BlockSpec((PEER_TOK, w), lambda i: (i, 0))
    smem = lambda imap: pl.BlockSpec((PEER_TOK, nk), imap, memory_space=pltpu.SMEM)
    return pl.pallas_call(
        functools.partial(_peer_gather_kernel, n=steps),
        grid=(steps,),
        in_specs=[smem(lambda i: (i, 0)), smem(lambda i: (jnp.minimum(i + 1, steps - 1), 0)),
                  blk(nk), blk(D_MODEL), blk(D_MODEL), pl.BlockSpec(memory_space=pl.ANY)],
        out_specs=blk(D_MODEL),
        out_shape=jax.ShapeDtypeStruct((n, D_MODEL), F32),
        scratch_shapes=[pltpu.VMEM((2, PEER_TOK * nk, D_MODEL), jnp.uint32), pltpu.SemaphoreType.DMA((2,))],
        compiler_params=_cparams(("arbitrary",)),
        name="peer_gather",
    )(e, e, g, h2, x2, uv)


def _permute_w_in(w_in):
    sizes = (D_INNER, CONV_DIM, SSM_HEADS, N_HEADS * HEAD_DIM, 3 * ROW_W, 3 * N_HEADS, 2 * D_MODEL)
    cuts = np.cumsum((0,) + sizes)
    z, xbc, dt, q, kv, ng, mg = (w_in[:, cuts[i]:cuts[i + 1]] for i in range(7))
    pad = jnp.zeros((D_MODEL, LANES - SSM_HEADS - 3 * N_HEADS), w_in.dtype)
    return jnp.concatenate([xbc, z, mg, q, kv, dt, ng, pad], axis=1).astype(BF16)


def _pack_experts(expert_u, expert_v):
    bits = lambda a: lax.bitcast_convert_type(a.astype(BF16), jnp.uint16).astype(jnp.uint32)
    return (bits(expert_v) << 16) | bits(expert_u)


def _layer_weights(norm1_w, w_in, conv_w, conv_b, dt_bias, a_log, d_skip, ssm_norm_w, w_proj_ssm, q_norm_w,
                   k_norm_w, cmp_pos, w_cmp, w_proj_attn, w_out, norm2_w, w_pq, sub_keys, expert_u, expert_v):
    wbd, pos, kg = _compress_weights(w_cmp, cmp_pos, k_norm_w[0])
    return dict(norm1_w=norm1_w, w_in=_permute_w_in(w_in), conv_w=conv_w, conv_b=conv_b, dt_bias=dt_bias,
                a_log=a_log, d_skip=d_skip, ssm_norm_w=ssm_norm_w, wps=w_proj_ssm.astype(BF16),
                q_norm_w=q_norm_w, k_norm_w=k_norm_w, wbd=wbd, pos=pos, kg=kg, wpa=w_proj_attn.astype(BF16),
                wo=w_out.astype(BF16), norm2_w=norm2_w, wpq=w_pq.astype(BF16), sub_keys=sub_keys.astype(BF16),
                uv=_pack_experts(expert_u, expert_v))


def _token_tail(x2d, y_ssm, o_attn, u2d, w):
    x2, h2, pq = _merge(x2d, y_ssm, o_attn, u2d, w["wps"], w["wpa"], w["wo"], w["norm2_w"], w["wpq"])
    e, g = _peer_select(pq, w["sub_keys"])
    return _peer_gather(e, g, h2, x2, w["uv"])


def _layer_prompt(x, w, win_buf):
    b, t = x.shape[:2]
    assert t % (2 * Q_BLOCK) == 0 and t >= WINDOW + Q_BLOCK and t >= win_buf
    x2d = x.reshape(b * t, D_MODEL)
    u2d = _inproj(x2d, w["norm1_w"], w["w_in"])
    u3 = u2d.reshape(b, t, D_IN_P)
    y_ssm, h_last = _ssd(u3, jnp.zeros((b, D_INNER, D_STATE), F32), jnp.zeros((b, 8, CONV_DIM), F32),
                         w["conv_w"], w["conv_b"], w["dt_bias"], w["a_log"], w["d_skip"], w["ssm_norm_w"])
    cos_t, sin_t = _rope_tables(jnp.arange(t, dtype=jnp.int32))
    qn, qr, slc, win = _nsa_prep(u2d, cos_t, sin_t, w["q_norm_w"], w["k_norm_w"], t)
    cmp_rows = u3[:, :, C_KV:C_KV + ROW_W]
    kc, vc = _compress_prompt(cmp_rows, w["wbd"], w["pos"], w["kg"])
    hw = N_HEADS * HEAD_DIM
    o_attn = _nsa_prompt(qn.reshape(b, t, hw), qr.reshape(b, t, hw), u3, kc, vc,
                         slc.reshape(b, t, ROW_W), win.reshape(b, t, ROW_W))
    y = _token_tail(x2d, y_ssm.reshape(b * t, D_INNER), o_attn.reshape(b * t, hw), u2d, w)
    rows = lambda a: a.reshape(b, -1, 2, N_KV, HEAD_DIM)
    state = (rows(cmp_rows), rows(slc), rows(win.reshape(b, t, ROW_W)[:, t - win_buf:]),
             h_last.reshape(b, SSM_HEADS, SSM_HEADDIM, D_STATE), u3[:, t - (CONV_W - 1):, C_XBC:C_XBC + CONV_DIM])
    return y.reshape(b, t, D_MODEL), state


def _layer_sample(x, w, cache_cmp, cache_slc, page_table, win_state, h0, conv_state):
    b, t = x.shape[:2]
    assert t == 1
    past = page_table.shape[1] * PAGE_SIZE
    win_buf = win_state.shape[1]
    assert page_table.shape[1] % N_PARTS == 0 and win_buf >= 1
    x2d = x.reshape(b, D_MODEL)
    u2d = _inproj(x2d, w["norm1_w"], w["w_in"])
    u3 = u2d.reshape(b, 1, D_IN_P)
    prefix8 = jnp.pad(conv_state, ((0, 0), (8 - (CONV_W - 1), 0), (0, 0)))
    y_ssm, h_last = _ssd(u3, h0.reshape(b, D_INNER, D_STATE), prefix8, w["conv_w"], w["conv_b"], w["dt_bias"],
                         w["a_log"], w["d_skip"], w["ssm_norm_w"])
    cos_t, sin_t = _rope_tables(jnp.full((1,), past, jnp.int32))
    qn, qr, slc, win = _nsa_prep(u2d, cos_t, sin_t, w["q_norm_w"], w["k_norm_w"], 1)
    n_pool = cache_cmp.shape[0]
    page_t = lambda cache: jnp.transpose(cache, (0, 2, 3, 4, 1)).reshape(n_pool, ROW_W, PAGE_SIZE)
    o_attn = _nsa_sample_t(qn, qr, u2d, slc, win, win_state.reshape(b, win_buf, ROW_W), page_table,
                           page_t(cache_cmp), page_t(cache_slc), w["wbd"], w["pos"], w["kg"])
    y = _token_tail(x2d, y_ssm.reshape(b, D_INNER), o_attn, u2d, w)
    rows = lambda a: a.reshape(b, -1, 2, N_KV, HEAD_DIM)
    new_win = jnp.concatenate([win_state.reshape(b, win_buf, ROW_W), win.reshape(b, 1, ROW_W)], axis=1)[:, 1:]
    new_conv = jnp.concatenate([conv_state, u3[:, :, C_XBC:C_XBC + CONV_DIM]], axis=1)[:, 1:]
    state = (rows(u2d[:, C_KV:C_KV + ROW_W]), rows(slc), rows(new_win),
             h_last.reshape(b, SSM_HEADS, SSM_HEADDIM, D_STATE), new_conv)
    return y.reshape(b, 1, D_MODEL), state


def kernel(x_prompt, x_sample, cache_cmp_kv, cache_slc_kv, page_table, state_win_kv, state_ssm, state_conv,
           norm1_w, w_in, conv_w, conv_b, dt_bias, a_log, d_skip, ssm_norm_w, w_proj_ssm, q_norm_w, k_norm_w,
           cmp_pos, w_cmp, w_proj_attn, w_out, norm2_w, w_pq, sub_keys, expert_u, expert_v):
    depth = w_in.shape[0]
    win_buf = state_win_kv.shape[2]
    yp, ys = x_prompt, x_sample
    new_p, new_s = [], []
    for l in range(depth):
        w = _layer_weights(norm1_w[l], w_in[l], conv_w[l], conv_b[l], dt_bias[l], a_log[l], d_skip[l],
                           ssm_norm_w[l], w_proj_ssm[l], q_norm_w[l], k_norm_w[l], cmp_pos[l], w_cmp[l],
                           w_proj_attn[l], w_out[l], norm2_w[l], w_pq[l], sub_keys[l], expert_u[l], expert_v[l])
        yp, st_p = _layer_prompt(yp, w, win_buf)
        ys, st_s = _layer_sample(ys, w, cache_cmp_kv[l], cache_slc_kv[l], page_table, state_win_kv[l],
                                 state_ssm[l], state_conv[l])
        new_p.append(st_p)
        new_s.append(st_s)
    stack = lambda states, i: jnp.stack([s[i] for s in states], axis=0)
    return (yp, ys, stack(new_p, 0), stack(new_s, 0), stack(new_p, 1), stack(new_s, 1), stack(new_p, 2),
            stack(new_s, 2), stack(new_p, 3), stack(new_s, 3), stack(new_p, 4), stack(new_s, 4))
```

```python
import functools
import math

import jax
import jax.numpy as jnp
import numpy as np
from jax import lax
from jax.experimental import pallas as pl
from jax.experimental.pallas import tpu as pltpu

F32 = jnp.float32
BF16 = jnp.bfloat16
HI = lax.Precision.HIGHEST

D_MODEL = 1024
PAGE_SIZE = 128
D_INNER = 2048
SSM_HEADDIM = 64
SSM_HEADS = 32
SSM_GROUPS = 8
SSM_HPG = 4
D_STATE = 128
CONV_W = 4
CONV_DIM = 4096
SSD_CHUNK = 128
N_HEADS = 16
N_KV = 4
GQA = 4
HEAD_DIM = 64
CMP_STRIDE = 16
CMP_LEN = 32
SLC_BLOCK = 64
SUB_PER_SLC = 4
N_SEL = 16
WINDOW = 512
Q_BLOCK = 128
ROPE_THETA = 10000.0
ATTN_SCALE = HEAD_DIM ** -0.5
PEER_HEADS = 8
N_KEYS = 128
PEER_TOPK = 16
PEER_DK = 128
EPS = 1e-6
NEG = -1e9
FORCE_BONUS = 1e3

LANES = 128
KV_W = N_KV * HEAD_DIM
ROW_W = 2 * KV_W
ROW_TILES = ROW_W // LANES
C_XBC, C_Z, C_MERGE, C_Q, C_KV, C_SMALL = 0, 4096, 6144, 8192, 9216, 10752
D_IN_P = 10880
IN_TILE_N = 2176
VMEM_LIMIT = 56 * 1024 * 1024


def _cparams(sem, vmem=VMEM_LIMIT):
    return pltpu.CompilerParams(dimension_semantics=sem, vmem_limit_bytes=vmem)


def _nt(a, b, precision=None):
    return lax.dot_general(a, b, (((1,), (1,)), ((), ())), preferred_element_type=F32, precision=precision)


def _tn(a, b, precision=None):
    return lax.dot_general(a, b, (((0,), (0,)), ((), ())), preferred_element_type=F32, precision=precision)


def _mm(a, b, precision=None):
    return jnp.dot(a, b, preferred_element_type=F32, precision=precision)


def _sigmoid(x):
    return 1.0 / (1.0 + jnp.exp(-x))


def _silu(x):
    return x * _sigmoid(x)


def _softplus(x):
    return jnp.maximum(x, 0.0) + jnp.log1p(jnp.exp(-jnp.abs(x)))


def _inproj_kernel(x_ref, nw_ref, w_ref, o_ref):
    x = x_ref[...]
    h = x * lax.rsqrt(jnp.mean(x * x, axis=-1, keepdims=True) + EPS) * nw_ref[...]
    o_ref[...] = _mm(h.astype(BF16), w_ref[...])


def _inproj(x2d, norm_w, w_p):
    n = x2d.shape[0]
    tm = min(512, n)
    return pl.pallas_call(
        _inproj_kernel,
        grid=(D_IN_P // IN_TILE_N, n // tm),
        in_specs=[pl.BlockSpec((tm, D_MODEL), lambda j, i: (i, 0)),
                  pl.BlockSpec((1, D_MODEL), lambda j, i: (0, 0)),
                  pl.BlockSpec((D_MODEL, IN_TILE_N), lambda j, i: (0, j))],
        out_specs=pl.BlockSpec((tm, IN_TILE_N), lambda j, i: (i, j)),
        out_shape=jax.ShapeDtypeStruct((n, D_IN_P), F32),
        compiler_params=_cparams(("arbitrary", "arbitrary")),
        name="inproj",
    )(x2d, norm_w.reshape(1, D_MODEL), w_p)


def _ssd_kernel(xbc_ref, z_ref, sm_ref, h0_ref, pre_ref, cw_ref, cb_ref, dtb_ref, alog_ref, dsk_ref, nw_ref,
                y_ref, hl_ref, st_ref, xe_ref, *, q, qp):
    c = pl.program_id(1)

    @pl.when(c == 0)
    def _():
        st_ref[...] = h0_ref[0]
        xe_ref[0:8, :] = pre_ref[0]

    @pl.when(c > 0)
    def _():
        xe_ref[0:8, :] = xe_ref[qp:qp + 8, :]

    xe_ref[8:8 + q, :] = xbc_ref[0]
    if qp > q:
        xe_ref[8 + q:8 + qp, :] = jnp.zeros((qp - q, CONV_DIM), F32)

    conv = cb_ref[...] + xe_ref[8:8 + qp, :] * cw_ref[3:4, :]
    for s in range(1, CONV_W):
        conv = conv + xe_ref[8 - s:8 - s + qp, :] * cw_ref[3 - s:4 - s, :]
    act = _silu(conv)

    lane = lax.broadcasted_iota(jnp.int32, (1, LANES), 1)
    head_lane = lane < SSM_HEADS
    sm = sm_ref[0]
    if qp > q:
        sm = jnp.concatenate([sm, jnp.zeros((qp - q, LANES), F32)], axis=0)
    dt = _softplus(sm + dtb_ref[...])
    dt = jnp.where(head_lane, dt, 0.0)
    if qp > q:
        row = lax.broadcasted_iota(jnp.int32, (qp, 1), 0)
        dt = jnp.where(row < q, dt, 0.0)
    a = jnp.where(head_lane, -jnp.exp(alog_ref[...]), 0.0)
    da = dt * a
    ri = lax.broadcasted_iota(jnp.int32, (qp, qp), 0)
    ci = lax.broadcasted_iota(jnp.int32, (qp, qp), 1)
    causal = ci <= ri
    cs = _mm(causal.astype(F32), da, precision=HI)
    eye = (lax.broadcasted_iota(jnp.int32, (LANES, LANES), 0)
           == lax.broadcasted_iota(jnp.int32, (LANES, LANES), 1)).astype(F32)
    cs_t = _nt(eye, cs, precision=HI)
    cs_last = cs[qp - 1:qp, :]
    e_cs = jnp.exp(cs)
    e_end = jnp.exp(cs_last - cs)
    e_last = jnp.exp(cs_last)

    for g in range(SSM_GROUPS):
        bm = act[:, D_INNER + g * D_STATE:D_INNER + (g + 1) * D_STATE].astype(BF16)
        cm = act[:, D_INNER + SSM_GROUPS * D_STATE + g * D_STATE:
                 D_INNER + SSM_GROUPS * D_STATE + (g + 1) * D_STATE].astype(BF16)
        cbm = _nt(cm, bm)
        r0 = g * SSM_HPG * SSM_HEADDIM
        s_g = st_ref[r0:r0 + SSM_HPG * SSM_HEADDIM, :]
        y_off = _nt(cm, s_g.astype(BF16))
        ys, xds, decs = [], [], []
        for r in range(SSM_HPG):
            h = g * SSM_HPG + r
            xs = act[:, h * SSM_HEADDIM:(h + 1) * SSM_HEADDIM]
            col = cs[:, h:h + 1]
            rowv = cs_t[h:h + 1, :]
            lmat = jnp.exp(jnp.where(causal, col - rowv, -jnp.inf))
            xdt = xs * dt[:, h:h + 1]
            y_d = _mm((cbm * lmat).astype(BF16), xdt.astype(BF16))
            ys.append(y_d + y_off[:, r * SSM_HEADDIM:(r + 1) * SSM_HEADDIM] * e_cs[:, h:h + 1]
                      + xs * dsk_ref[:, h:h + 1])
            xds.append(xdt * e_end[:, h:h + 1])
            decs.append(jnp.broadcast_to(e_last[:, h:h + 1], (SSM_HEADDIM, 1)))
        xd = jnp.concatenate(xds, axis=1)
        new = _tn(xd.astype(BF16), bm)
        dec = jnp.concatenate(decs, axis=0)
        st_ref[r0:r0 + SSM_HPG * SSM_HEADDIM, :] = dec * s_g + new
        yg = jnp.concatenate(ys, axis=1)
        w = D_INNER // SSM_GROUPS
        zg = z_ref[0][:, g * w:(g + 1) * w]
        if qp > q:
            yg = yg[:q]
        yg = yg * _silu(zg)
        yg = yg * lax.rsqrt(jnp.mean(yg * yg, axis=-1, keepdims=True) + EPS) * nw_ref[:, g * w:(g + 1) * w]
        y_ref[0, :, g * w:(g + 1) * w] = yg

    @pl.when(c == pl.num_programs(1) - 1)
    def _():
        hl_ref[0] = st_ref[...]


def _pad_lanes(v, width=LANES):
    v = v.reshape(1, -1)
    return jnp.pad(v, ((0, 0), (0, width - v.shape[1])))


def _ssd(u3, h0, prefix8, conv_w, conv_b, dt_bias, a_log, d_skip, ssm_norm_w):
    b, t = u3.shape[:2]
    q = min(SSD_CHUNK, t)
    qp = max(q, 8)
    nc = t // q
    row = lambda shape: pl.BlockSpec(shape, lambda i, c: (0, 0))
    return pl.pallas_call(
        functools.partial(_ssd_kernel, q=q, qp=qp),
        grid=(b, nc),
        in_specs=[pl.BlockSpec((1, q, CONV_DIM), lambda i, c: (i, c, C_XBC // CONV_DIM)),
                  pl.BlockSpec((1, q, D_INNER), lambda i, c: (i, c, C_Z // D_INNER)),
                  pl.BlockSpec((1, q, LANES), lambda i, c: (i, c, C_SMALL // LANES)),
                  pl.BlockSpec((1, D_INNER, D_STATE), lambda i, c: (i, 0, 0)),
                  pl.BlockSpec((1, 8, CONV_DIM), lambda i, c: (i, 0, 0)),
                  row((CONV_W, CONV_DIM)), row((1, CONV_DIM)), row((1, LANES)), row((1, LANES)), row((1, LANES)),
                  row((1, D_INNER))],
        out_specs=[pl.BlockSpec((1, q, D_INNER), lambda i, c: (i, c, 0)),
                   pl.BlockSpec((1, D_INNER, D_STATE), lambda i, c: (i, 0, 0))],
        out_shape=[jax.ShapeDtypeStruct((b, t, D_INNER), F32),
                   jax.ShapeDtypeStruct((b, D_INNER, D_STATE), F32)],
        scratch_shapes=[pltpu.VMEM((D_INNER, D_STATE), F32), pltpu.VMEM((qp + 8, CONV_DIM), F32)],
        compiler_params=_cparams(("arbitrary", "arbitrary")),
        name="ssd",
    )(u3, u3, u3, h0, prefix8, conv_w, conv_b.reshape(1, CONV_DIM), _pad_lanes(dt_bias), _pad_lanes(a_log),
      _pad_lanes(d_skip), ssm_norm_w.reshape(1, D_INNER))


def _head_sumsq(x, bd):
    sq = x * x
    hi = sq.astype(BF16)
    lo = (sq - hi.astype(F32)).astype(BF16)
    return _mm(hi, bd) + _mm(lo, bd)


def _rope_tiles(x, cos, sin_signed):
    n_tiles = x.shape[1] // LANES
    lane = lax.broadcasted_iota(jnp.int32, (1, LANES), 1)
    first_half = (lane % HEAD_DIM) < (HEAD_DIM // 2)
    outs = []
    for i in range(n_tiles):
        xt = x[:, i * LANES:(i + 1) * LANES]
        swapped = jnp.where(first_half, pltpu.roll(xt, LANES - HEAD_DIM // 2, 1), pltpu.roll(xt, HEAD_DIM // 2, 1))
        outs.append(xt * cos + swapped * sin_signed)
    return jnp.concatenate(outs, axis=1)


def _nsa_prep_kernel(q_ref, kv_ref, cos_ref, sin_ref, bd_ref, qw_ref, kw_ref, qn_ref, qr_ref, slc_ref, win_ref):
    cos = cos_ref[...]
    sin = sin_ref[...]
    bd = bd_ref[...]
    q = q_ref[...]
    qn = q * lax.rsqrt(_head_sumsq(q, bd) * (1.0 / HEAD_DIM) + EPS) * qw_ref[...]
    qn_ref[...] = qn
    qr_ref[...] = _rope_tiles(qn, cos, sin)
    for br, o_ref in ((1, slc_ref), (2, win_ref)):
        k = kv_ref[:, br * ROW_W:br * ROW_W + KV_W]
        kn = k * lax.rsqrt(_head_sumsq(k, bd[:KV_W, :KV_W]) * (1.0 / HEAD_DIM) + EPS) * kw_ref[br:br + 1, :]
        o_ref[:, :KV_W] = _rope_tiles(kn, cos, sin)
        o_ref[:, KV_W:] = kv_ref[:, br * ROW_W + KV_W:(br + 1) * ROW_W]


def _nsa_prep(u2d, cos_t, sin_t, q_norm_w, k_norm_w, t):
    n = u2d.shape[0]
    tm = min(512, n, t) if t > 1 else n
    tab_blocks = max(t // tm, 1)
    if t == 1:
        cos_t = jnp.broadcast_to(cos_t, (tm, LANES))
        sin_t = jnp.broadcast_to(sin_t, (tm, LANES))
    head_id = np.arange(N_HEADS * HEAD_DIM) // HEAD_DIM
    bd = jnp.asarray(head_id[:, None] == head_id[None, :], BF16)
    qw = jnp.tile(q_norm_w, N_HEADS).reshape(1, -1)
    kw = jnp.tile(k_norm_w, (1, N_KV))
    tab = lambda: pl.BlockSpec((tm, LANES), lambda i: (i % tab_blocks, 0))
    return pl.pallas_call(
        _nsa_prep_kernel,
        grid=(n // tm,),
        in_specs=[pl.BlockSpec((tm, N_HEADS * HEAD_DIM), lambda i: (i, C_Q // (N_HEADS * HEAD_DIM))),
                  pl.BlockSpec((tm, 3 * ROW_W), lambda i: (i, C_KV // (3 * ROW_W))),
                  tab(), tab(),
                  pl.BlockSpec((N_HEADS * HEAD_DIM, N_HEADS * HEAD_DIM), lambda i: (0, 0)),
                  pl.BlockSpec((1, N_HEADS * HEAD_DIM), lambda i: (0, 0)),
                  pl.BlockSpec((3, KV_W), lambda i: (0, 0))],
        out_specs=[pl.BlockSpec((tm, N_HEADS * HEAD_DIM), lambda i: (i, 0)),
                   pl.BlockSpec((tm, N_HEADS * HEAD_DIM), lambda i: (i, 0)),
                   pl.BlockSpec((tm, ROW_W), lambda i: (i, 0)),
                   pl.BlockSpec((tm, ROW_W), lambda i: (i, 0))],
        out_shape=[jax.ShapeDtypeStruct((n, N_HEADS * HEAD_DIM), F32),
                   jax.ShapeDtypeStruct((n, N_HEADS * HEAD_DIM), F32),
                   jax.ShapeDtypeStruct((n, ROW_W), F32),
                   jax.ShapeDtypeStruct((n, ROW_W), F32)],
        compiler_params=_cparams(("arbitrary",)),
        name="nsa_prep",
    )(u2d, u2d, cos_t, sin_t, bd, qw, kw)


def _rope_tables(pos):
    half = HEAD_DIM // 2
    inv = ROPE_THETA ** (-jnp.arange(half, dtype=F32) / half)
    ang = pos.astype(F32)[:, None] * inv[None, :]
    cos = jnp.cos(ang)
    sin = jnp.sin(ang)
    reps = LANES // HEAD_DIM
    return (jnp.tile(jnp.concatenate([cos, cos], axis=1), (1, reps)),
            jnp.tile(jnp.concatenate([-sin, sin], axis=1), (1, reps)))


def _sub_block_proj(load_rows, wbd_ref, pos_ref):
    acc = [[None, None], [None, None]]
    per = KV_W // LANES
    for l in range(CMP_STRIDE):
        for cidx in range(2):
            rows = jnp.concatenate([load_rows(l, cidx * per + j) for j in range(per)], axis=1)
            for half in range(2):
                x = rows + pos_ref[half * CMP_STRIDE + l, cidx:cidx + 1, :]
                part = _mm(x.astype(BF16), wbd_ref[half * CMP_STRIDE + l, cidx])
                acc[half][cidx] = part if acc[half][cidx] is None else acc[half][cidx] + part
    return acc


def _compress_finish(first_k, first_v, second_k, second_v, kg):
    n_sub = first_k.shape[0]
    k = first_k + pltpu.roll(second_k, n_sub - 1, 0)
    v = first_v + pltpu.roll(second_v, n_sub - 1, 0)
    parts = []
    for hd in range(N_KV):
        kh = k[:, hd * HEAD_DIM:(hd + 1) * HEAD_DIM]
        parts.append(kh * lax.rsqrt(jnp.mean(kh * kh, axis=-1, keepdims=True) + EPS))
    return jnp.concatenate(parts, axis=1) * kg, v


def _compress_kernel(rows_ref, wbd_ref, pos_ref, kg_ref, kc_ref, vc_ref, *, n_sub):
    acc = _sub_block_proj(lambda l, j: rows_ref[0, pl.ds(l * ROW_TILES + j, n_sub, stride=CMP_STRIDE * ROW_TILES), :],
                          wbd_ref, pos_ref)
    kc, vc = _compress_finish(acc[0][0], acc[0][1], acc[1][0], acc[1][1], kg_ref[...])
    kc_ref[0] = kc
    vc_ref[0] = vc


def _compress_weights(w_cmp, cmp_pos, k_gain):
    eye = jnp.eye(N_KV, dtype=F32)
    wbd = jnp.einsum("hg,lcde->lchdge", eye, w_cmp).reshape(CMP_LEN, 2, KV_W, KV_W).astype(BF16)
    pos = jnp.tile(cmp_pos, (1, 1, N_KV))
    kg = jnp.tile(k_gain, N_KV).reshape(1, KV_W)
    return wbd, pos, kg


def _compress_prompt(cmp_rows, wbd, pos, kg):
    b, t = cmp_rows.shape[:2]
    cmp_rows = cmp_rows.reshape(b, t * ROW_TILES, LANES)
    n_sub = t // CMP_STRIDE
    const = lambda shape: pl.BlockSpec(shape, lambda i: (0,) * len(shape))
    return pl.pallas_call(
        functools.partial(_compress_kernel, n_sub=n_sub),
        grid=(b,),
        in_specs=[pl.BlockSpec((1, t * ROW_TILES, LANES), lambda i: (i, 0, 0)),
                  const((CMP_LEN, 2, KV_W, KV_W)), const((CMP_LEN, 2, KV_W)), const((1, KV_W))],
        out_specs=[pl.BlockSpec((1, n_sub, KV_W), lambda i: (i, 0, 0)),
                   pl.BlockSpec((1, n_sub, KV_W), lambda i: (i, 0, 0))],
        out_shape=[jax.ShapeDtypeStruct((b, n_sub, KV_W), F32), jax.ShapeDtypeStruct((b, n_sub, KV_W), F32)],
        compiler_params=_cparams(("arbitrary",)),
        name="compress_prompt",
    )(cmp_rows, wbd, pos, kg)


def _softmax_rows(s, mask):
    s = jnp.where(mask, s, NEG)
    m = jnp.max(s, axis=-1, keepdims=True)
    e = jnp.exp(s - m)
    p = e / jnp.sum(e, axis=-1, keepdims=True)
    return jnp.where(mask, p, 0.0)


def _stack_heads(x, kv):
    return jnp.concatenate([x[:, (kv * GQA + g) * HEAD_DIM:(kv * GQA + g + 1) * HEAD_DIM] for g in range(GQA)],
                           axis=0)


def _block_scores_t(p, tq, n_cmp, n_blk, sel_mat_t):
    imp = p[0:tq]
    for g in range(1, GQA):
        imp = imp + p[g * tq:(g + 1) * tq]
    return _nt(sel_mat_t, imp, precision=HI)


def _top_rank_mask_t(score_t, n_blk):
    jidx = lax.broadcasted_iota(jnp.int32, score_t.shape, 0)
    cnt = jnp.zeros(score_t.shape, F32)
    for j in range(n_blk):
        rowv = score_t[j:j + 1, :]
        beats = (rowv > score_t) | ((rowv == score_t) & (j < jidx))
        cnt = cnt + beats.astype(F32)
    return cnt < float(N_SEL)


def _nsa_prompt_kernel(qn_ref, qr_ref, sm_ref, kc_ref, vc_ref, slc_ref, win_ref, selm_ref, exp_ref, o_ref,
                       selk_ref, *, t, n_cmp, n_blk):
    tq = Q_BLOCK
    n = pl.program_id(1)
    start = n * tq
    qn = qn_ref[0]
    qr = qr_ref[0]
    gates = _sigmoid(sm_ref[0])
    q_pos1 = start + lax.broadcasted_iota(jnp.int32, (tq, 1), 0)
    q_pos = jnp.concatenate([q_pos1] * GQA, axis=0)
    ci = lax.broadcasted_iota(jnp.int32, (1, n_cmp), 1)
    c_mask = (ci * CMP_STRIDE + (CMP_LEN - 1)) <= q_pos
    jcol = lax.broadcasted_iota(jnp.int32, (n_blk, 1), 0)
    qrow = start + lax.broadcasted_iota(jnp.int32, (1, tq), 1)
    cur = qrow // SLC_BLOCK
    forced_t = (jcol == 0) | (jcol == cur) | (jcol == cur - 1)
    valid_t = jcol * SLC_BLOCK <= qrow
    w_lo = jnp.maximum(start - WINDOW, 0)
    w_len = WINDOW + tq
    kp_w = w_lo + lax.broadcasted_iota(jnp.int32, (1, w_len), 1)
    w_mask = (kp_w <= q_pos) & (kp_w > q_pos - WINDOW)
    tk = 2 * tq
    n_tiles = (n + 2) // 2
    kp_all = lax.broadcasted_iota(jnp.int32, (1, t), 1)
    eye_q = (lax.broadcasted_iota(jnp.int32, (tq, tq), 0)
             == lax.broadcasted_iota(jnp.int32, (tq, tq), 1)).astype(BF16)

    outs = []
    for kv in range(N_KV):
        qn_s = _stack_heads(qn, kv).astype(BF16)
        qr_s = (_stack_heads(qr, kv) * ATTN_SCALE).astype(BF16)
        lanes = slice(kv * HEAD_DIM, (kv + 1) * HEAD_DIM)
        vlanes = slice(KV_W + kv * HEAD_DIM, KV_W + (kv + 1) * HEAD_DIM)
        kc = kc_ref[0][:, lanes].astype(BF16)
        vc = vc_ref[0][:, lanes].astype(BF16)
        p = _softmax_rows(_nt(qn_s, kc) * ATTN_SCALE, c_mask)
        o_cmp = _mm(p.astype(BF16), vc)
        score_t = _block_scores_t(p, tq, n_cmp, n_blk, selm_ref[...])
        score_t = jnp.where(forced_t, score_t + FORCE_BONUS, score_t)
        score_t = jnp.where(valid_t, score_t, NEG)
        sel_t = _top_rank_mask_t(score_t, n_blk) & valid_t
        sel = _nt(eye_q, jnp.where(sel_t, 1.0, 0.0).astype(BF16))
        sel_keys = _mm(sel.astype(BF16), exp_ref[...])
        selk_ref[...] = jnp.where((sel_keys > 0.5) & (kp_all <= q_pos1), 0.0, NEG)

        def tile_body(i, carry, lanes=lanes, vlanes=vlanes, qr_s=qr_s):
            m, l, acc = carry
            k0 = pl.multiple_of(i * tk, tk)
            kt = slc_ref[0, pl.ds(k0, tk), lanes].astype(BF16)
            vt = slc_ref[0, pl.ds(k0, tk), vlanes].astype(BF16)
            bias = selk_ref[:, pl.ds(k0, tk)]
            s = _nt(qr_s, kt) + jnp.concatenate([bias] * GQA, axis=0)
            m_new = jnp.maximum(m, jnp.max(s, axis=-1, keepdims=True))
            alpha = jnp.exp(m - m_new)
            e = jnp.exp(s - m_new)
            return (m_new, alpha * l + jnp.sum(e, axis=-1, keepdims=True),
                    alpha * acc + _mm(e.astype(BF16), vt))

        m0 = jnp.full((GQA * tq, 1), NEG, F32)
        l0 = jnp.zeros((GQA * tq, 1), F32)
        a0 = jnp.zeros((GQA * tq, HEAD_DIM), F32)
        _, l_s, acc_s = lax.fori_loop(0, n_tiles, tile_body, (m0, l0, a0))
        o_slc = acc_s / l_s
        kw = win_ref[0, pl.ds(pl.multiple_of(w_lo, tq), w_len), lanes].astype(BF16)
        vw = win_ref[0, pl.ds(pl.multiple_of(w_lo, tq), w_len), vlanes].astype(BF16)
        pw = _softmax_rows(_nt(qr_s, kw), w_mask)
        o_win = _mm(pw.astype(BF16), vw)
        for g in range(GQA):
            hd = kv * GQA + g
            rows = slice(g * tq, (g + 1) * tq)
            g0 = gates[:, SSM_HEADS + hd:SSM_HEADS + hd + 1]
            g1 = gates[:, SSM_HEADS + N_HEADS + hd:SSM_HEADS + N_HEADS + hd + 1]
            g2 = gates[:, SSM_HEADS + 2 * N_HEADS + hd:SSM_HEADS + 2 * N_HEADS + hd + 1]
            outs.append(g0 * o_cmp[rows] + g1 * o_slc[rows] + g2 * o_win[rows])
    o_ref[0] = jnp.concatenate(outs, axis=1)


def _selection_matrices(n_cmp, n_blk, n_keys):
    i = np.arange(n_cmp)[None, :]
    j = np.arange(n_blk)[:, None]
    sel = ((i >= SUB_PER_SLC * j - 1) & (i <= SUB_PER_SLC * j + SUB_PER_SLC - 1)).astype(np.float32)
    key = np.arange(n_keys)[None, :]
    expand = (key // SLC_BLOCK == j).astype(np.float32)
    return jnp.asarray(sel), jnp.asarray(expand, BF16)


def _nsa_prompt(qn3, qr3, u3, kc, vc, slc3, win3):
    b, t = qn3.shape[:2]
    n_cmp = t // CMP_STRIDE
    n_blk = t // SLC_BLOCK
    selm, expand = _selection_matrices(n_cmp, n_blk, t)
    selm = selm.at[:, n_cmp - 1].set(0.0)
    hw = N_HEADS * HEAD_DIM
    full = lambda w: pl.BlockSpec((1, t, w), lambda i, n: (i, 0, 0))
    return pl.pallas_call(
        functools.partial(_nsa_prompt_kernel, t=t, n_cmp=n_cmp, n_blk=n_blk),
        grid=(b, t // Q_BLOCK),
        in_specs=[pl.BlockSpec((1, Q_BLOCK, hw), lambda i, n: (i, n, 0)),
                  pl.BlockSpec((1, Q_BLOCK, hw), lambda i, n: (i, n, 0)),
                  pl.BlockSpec((1, Q_BLOCK, LANES), lambda i, n: (i, n, C_SMALL // LANES)),
                  pl.BlockSpec((1, n_cmp, KV_W), lambda i, n: (i, 0, 0)),
                  pl.BlockSpec((1, n_cmp, KV_W), lambda i, n: (i, 0, 0)),
                  full(ROW_W), full(ROW_W),
                  pl.BlockSpec((n_blk, n_cmp), lambda i, n: (0, 0)),
                  pl.BlockSpec((n_blk, t), lambda i, n: (0, 0))],
        out_specs=pl.BlockSpec((1, Q_BLOCK, hw), lambda i, n: (i, n, 0)),
        out_shape=jax.ShapeDtypeStruct((b, t, hw), F32),
        scratch_shapes=[pltpu.VMEM((Q_BLOCK, t), F32)],
        compiler_params=_cparams(("arbitrary", "arbitrary")),
        name="nsa_prompt",
    )(qn3, qr3, u3, kc, vc, slc3, win3, selm, expand)


N_PARTS = 2


def _head_block_diag(x_ref, scale):
    q16 = jnp.concatenate([x_ref[0, :, hd * HEAD_DIM:(hd + 1) * HEAD_DIM] for hd in range(N_HEADS)], axis=0) * scale
    kv_of_row = lax.broadcasted_iota(jnp.int32, (N_HEADS, HEAD_DIM), 0) // GQA
    return jnp.concatenate([jnp.where(kv_of_row == kv, q16, 0.0) for kv in range(N_KV)], axis=1)


def _nsa_sample_t_kernel(pt_ref, qn_ref, qr_ref, sm_ref, slcn_ref, winn_ref, wins_ref, wbd_ref, pos_ref, kg_ref,
                         selm_ref, cmp_hbm, slc_hbm, o_ref,
                         buf_ref, sem_ref, rows_ref, fs_ref, sel_ref, ocmp_ref, m_ref, l_ref, acc_ref,
                         *, past, ppp, n_blk_pad, batch):
    b = pl.program_id(0)
    s = pl.program_id(1)
    n_steps = 2 * N_PARTS
    c = b * n_steps + s
    total = batch * n_steps
    rp = ppp * PAGE_SIZE
    nsp = rp // CMP_STRIDE
    n_sub = N_PARTS * nsp
    win_buf = wins_ref.shape[1]
    cur = past // SLC_BLOCK

    def page_copy(hbm, bb, part, pg, slot):
        return pltpu.make_async_copy(hbm.at[pt_ref[bb, part * ppp + pg]],
                                     buf_ref.at[slot, pl.ds(pg * ROW_W, ROW_W)], sem_ref.at[slot])

    def start_chunk(cc, slot):
        bb = cc // n_steps
        ss = cc % n_steps

        @pl.when(ss < N_PARTS)
        def _():
            for pg in range(ppp):
                page_copy(cmp_hbm, bb, ss, pg, slot).start()

        @pl.when(ss >= N_PARTS)
        def _():
            for pg in range(ppp):
                page_copy(slc_hbm, bb, ss - N_PARTS, pg, slot).start()

    @pl.when(c == 0)
    def _():
        start_chunk(c, 0)

    @pl.when(c + 1 < total)
    def _():
        start_chunk(c + 1, (c + 1) % 2)

    slot = c % 2
    for pg in range(ppp):
        page_copy(cmp_hbm, b, 0, pg, slot).wait()

    @pl.when(s < N_PARTS)
    def _():
        for pg in range(ppp):
            for j in range(ROW_TILES):
                tile = buf_ref[slot, pg * ROW_W + j * LANES:pg * ROW_W + (j + 1) * LANES, :]
                rows_ref[j, pg * PAGE_SIZE:(pg + 1) * PAGE_SIZE, :] = tile.T
        acc = _sub_block_proj(lambda l, j: rows_ref[j, pl.ds(l, nsp, stride=CMP_STRIDE), :], wbd_ref, pos_ref)
        row0 = pl.multiple_of(s * nsp, nsp)
        fs_ref[0, pl.ds(row0, nsp), :] = acc[0][0]
        fs_ref[1, pl.ds(row0, nsp), :] = acc[0][1]
        fs_ref[2, pl.ds(row0, nsp), :] = acc[1][0]
        fs_ref[3, pl.ds(row0, nsp), :] = acc[1][1]

    @pl.when(s == N_PARTS - 1)
    def _():
        kc, vc = _compress_finish(fs_ref[0], fs_ref[1], fs_ref[2], fs_ref[3], kg_ref[...])
        ci = lax.broadcasted_iota(jnp.int32, (1, n_sub), 1)
        c_mask = (ci * CMP_STRIDE + (CMP_LEN - 1)) <= past
        qn_bd = _head_block_diag(qn_ref, 1.0).astype(BF16)
        p = _softmax_rows(_nt(qn_bd, kc.astype(BF16)) * ATTN_SCALE, c_mask)
        ocmp_ref[...] = _mm(p.astype(BF16), vc.astype(BF16))
        imps = [p[kv * GQA:kv * GQA + 1] + p[kv * GQA + 1:kv * GQA + 2] + p[kv * GQA + 2:kv * GQA + 3]
                + p[kv * GQA + 3:kv * GQA + 4] for kv in range(N_KV)]
        imp = jnp.concatenate(imps + [jnp.zeros((8 - N_KV, n_sub), F32)], axis=0)
        score = _mm(imp, selm_ref[...], precision=HI)
        jrow = lax.broadcasted_iota(jnp.int32, (1, n_blk_pad), 1)
        forced = (jrow == 0) | (jrow == cur) | (jrow == cur - 1)
        valid = jrow * SLC_BLOCK <= past
        score = jnp.where(forced, score + FORCE_BONUS, score)
        score = jnp.where(valid, score, NEG)
        eye = (lax.broadcasted_iota(jnp.int32, (n_blk_pad, n_blk_pad), 0)
               == lax.broadcasted_iota(jnp.int32, (n_blk_pad, n_blk_pad), 1)).astype(F32)
        score_c = _nt(eye, score, precision=HI)
        jc = lax.broadcasted_iota(jnp.int32, (n_blk_pad, n_blk_pad), 0)
        jr = lax.broadcasted_iota(jnp.int32, (n_blk_pad, n_blk_pad), 1)
        sels = []
        for kv in range(N_KV):
            colv = score_c[:, kv:kv + 1]
            rowv = score[kv:kv + 1, :]
            beats = (colv > rowv) | ((colv == rowv) & (jc < jr))
            rank = jnp.sum(beats.astype(F32), axis=0, keepdims=True)
            sel = jnp.where((rank < float(N_SEL)) & valid, 1.0, 0.0)
            sels.extend([sel] * GQA)
        sel_ref[...] = jnp.concatenate(sels, axis=0)

    @pl.when(s >= N_PARTS)
    def _():
        part = s - N_PARTS

        @pl.when(part == 0)
        def _():
            m_ref[...] = jnp.full(m_ref.shape, NEG, F32)
            l_ref[...] = jnp.zeros(l_ref.shape, F32)
            acc_ref[...] = jnp.zeros(acc_ref.shape, F32)

        key = part * rp + lax.broadcasted_iota(jnp.int32, (n_blk_pad, rp), 1)
        blk = lax.broadcasted_iota(jnp.int32, (n_blk_pad, rp), 0)
        expand = jnp.where(key // SLC_BLOCK == blk, 1.0, 0.0).astype(BF16)
        ok = _mm(sel_ref[...].astype(BF16), expand) > 0.5
        qr_bd = _head_block_diag(qr_ref, ATTN_SCALE).astype(BF16)
        sc = jnp.concatenate([_mm(qr_bd, buf_ref[slot, pg * ROW_W:pg * ROW_W + KV_W, :].astype(BF16))
                              for pg in range(ppp)], axis=1)
        sc = jnp.where(ok, sc, NEG)
        m_old = m_ref[...]
        m_new = jnp.maximum(m_old, jnp.max(sc, axis=-1, keepdims=True))
        alpha = jnp.exp(m_old - m_new)
        e = jnp.where(ok, jnp.exp(sc - m_new), 0.0)
        m_ref[...] = m_new
        l_ref[...] = alpha * l_ref[...] + jnp.sum(e, axis=-1, keepdims=True)
        eb = e.astype(BF16)
        pv = None
        for pg in range(ppp):
            part_pv = _nt(eb[:, pg * PAGE_SIZE:(pg + 1) * PAGE_SIZE],
                          buf_ref[slot, pg * ROW_W + KV_W:(pg + 1) * ROW_W, :].astype(BF16))
            pv = part_pv if pv is None else pv + part_pv
        acc_ref[...] = alpha * acc_ref[...] + pv

    @pl.when(s == n_steps - 1)
    def _():
        qr_bd = _head_block_diag(qr_ref, ATTN_SCALE).astype(BF16).astype(F32)
        k_new = slcn_ref[0, :, :KV_W].astype(BF16).astype(F32)
        v_new = slcn_ref[0, :, KV_W:].astype(BF16).astype(F32)
        ok_new = sel_ref[:, cur:cur + 1] > 0.5
        s_new = jnp.where(ok_new, jnp.sum(qr_bd * k_new, axis=-1, keepdims=True), NEG)
        m_old = m_ref[...]
        m_new = jnp.maximum(m_old, s_new)
        alpha = jnp.exp(m_old - m_new)
        e_new = jnp.where(ok_new, jnp.exp(s_new - m_new), 0.0)
        l_s = alpha * l_ref[...] + e_new
        o_slc = (alpha * acc_ref[...] + e_new.astype(BF16).astype(F32) * v_new) / l_s
        kpos = past - win_buf + lax.broadcasted_iota(jnp.int32, (1, win_buf), 1)
        w_mask = (kpos <= past) & (kpos > past - WINDOW) & (kpos >= 0)
        sw = jnp.where(w_mask, _nt(qr_bd.astype(BF16), wins_ref[0, :, :KV_W].astype(BF16)), NEG)
        kw_new = winn_ref[0, :, :KV_W].astype(BF16).astype(F32)
        vw_new = winn_ref[0, :, KV_W:].astype(BF16).astype(F32)
        sw_new = jnp.sum(qr_bd * kw_new, axis=-1, keepdims=True)
        mw = jnp.maximum(jnp.max(sw, axis=-1, keepdims=True), sw_new)
        ew = jnp.where(w_mask, jnp.exp(sw - mw), 0.0)
        ew_new = jnp.exp(sw_new - mw)
        lw = jnp.sum(ew, axis=-1, keepdims=True) + ew_new
        o_win = (_mm((ew / lw).astype(BF16), wins_ref[0, :, KV_W:].astype(BF16))
                 + (ew_new / lw).astype(BF16).astype(F32) * vw_new)
        gates = _sigmoid(sm_ref[0])
        hrow = lax.broadcasted_iota(jnp.int32, (N_HEADS, LANES), 0)
        hlane = lax.broadcasted_iota(jnp.int32, (N_HEADS, LANES), 1)
        gcol = [jnp.sum(jnp.where(hlane == SSM_HEADS + br * N_HEADS + hrow, gates, 0.0), axis=-1, keepdims=True)
                for br in range(3)]
        mixed = gcol[0] * ocmp_ref[...] + gcol[1] * o_slc + gcol[2] * o_win
        o_ref[0] = jnp.concatenate(
            [mixed[hd:hd + 1, (hd // GQA) * HEAD_DIM:(hd // GQA + 1) * HEAD_DIM] for hd in range(N_HEADS)], axis=1)


def _nsa_sample_t(qn, qr, u2d, slc_new, win_new, win_state, page_table, cache_cmp_t, cache_slc_t, wbd, pos, kg):
    b = qn.shape[0]
    n_pages = page_table.shape[1]
    past = n_pages * PAGE_SIZE
    ppp = n_pages // N_PARTS
    n_sub = past // CMP_STRIDE
    n_blk = past // SLC_BLOCK + 1
    n_blk_pad = -(-n_blk // LANES) * LANES
    selm, _ = _selection_matrices(n_sub, n_blk_pad, 1)
    selm = selm.at[:, n_sub - 1].set(0.0).T
    hw = N_HEADS * HEAD_DIM
    win_buf = win_state.shape[1]
    n_steps = 2 * N_PARTS
    rp = ppp * PAGE_SIZE
    per_b = lambda w: pl.BlockSpec((1, 1, w), lambda i, s, pt: (i, 0, 0))
    const = lambda shape: pl.BlockSpec(shape, lambda i, s, pt: (0,) * len(shape))
    grid_spec = pltpu.PrefetchScalarGridSpec(
        num_scalar_prefetch=1,
        grid=(b, n_steps),
        in_specs=[per_b(hw), per_b(hw),
                  pl.BlockSpec((1, 1, LANES), lambda i, s, pt: (i, 0, C_SMALL // LANES)),
                  per_b(ROW_W), per_b(ROW_W),
                  pl.BlockSpec((1, win_buf, ROW_W), lambda i, s, pt: (i, 0, 0)),
                  const((CMP_LEN, 2, KV_W, KV_W)), const((CMP_LEN, 2, KV_W)), const((1, KV_W)),
                  const((n_sub, n_blk_pad)),
                  pl.BlockSpec(memory_space=pl.ANY), pl.BlockSpec(memory_space=pl.ANY)],
        out_specs=per_b(hw),
        scratch_shapes=[pltpu.VMEM((2, ppp * ROW_W, LANES), F32), pltpu.SemaphoreType.DMA((2,)),
                        pltpu.VMEM((ROW_TILES, rp, LANES), F32),
                        pltpu.VMEM((4, n_sub, KV_W), F32), pltpu.VMEM((N_HEADS, n_blk_pad), F32),
                        pltpu.VMEM((N_HEADS, KV_W), F32), pltpu.VMEM((N_HEADS, 1), F32),
                        pltpu.VMEM((N_HEADS, 1), F32), pltpu.VMEM((N_HEADS, KV_W), F32)])
    return pl.pallas_call(
        functools.partial(_nsa_sample_t_kernel, past=past, ppp=ppp, n_blk_pad=n_blk_pad, batch=b),
        grid_spec=grid_spec,
        out_shape=jax.ShapeDtypeStruct((b, 1, hw), F32),
        compiler_params=_cparams(("arbitrary", "arbitrary")),
        name="nsa_sample",
    )(page_table, qn.reshape(b, 1, hw), qr.reshape(b, 1, hw), u2d.reshape(b, 1, D_IN_P),
      slc_new.reshape(b, 1, ROW_W), win_new.reshape(b, 1, ROW_W), win_state, wbd, pos, kg, selm,
      cache_cmp_t, cache_slc_t).reshape(b, hw)


def _merge_kernel(x_ref, y_ref, o_ref, mg_ref, wps_ref, wpa_ref, wo_ref, n2_ref, wpq_ref, x2_ref, h2_ref, pq_ref):
    mg = _sigmoid(mg_ref[...])
    mixed = (mg[:, :D_MODEL] * _mm(y_ref[...].astype(BF16), wps_ref[...])
             + mg[:, D_MODEL:] * _mm(o_ref[...].astype(BF16), wpa_ref[...]))
    x2 = x_ref[...] + _mm(mixed.astype(BF16), wo_ref[...])
    x2_ref[...] = x2
    h2 = x2 * lax.rsqrt(jnp.mean(x2 * x2, axis=-1, keepdims=True) + EPS) * n2_ref[...]
    h2_ref[...] = h2
    pq_ref[...] = _mm(h2.astype(BF16), wpq_ref[...])


def _merge(x2d, y_ssm, o_attn, u2d, wps, wpa, wo, norm2_w, wpq):
    n = x2d.shape[0]
    tm = min(256, n)
    rows = lambda w, cb=0: pl.BlockSpec((tm, w), lambda i: (i, cb))
    const = lambda shape: pl.BlockSpec(shape, lambda i: (0, 0))
    out = jax.ShapeDtypeStruct((n, D_MODEL), F32)
    return pl.pallas_call(
        _merge_kernel,
        grid=(n // tm,),
        in_specs=[rows(D_MODEL), rows(D_INNER), rows(N_HEADS * HEAD_DIM), rows(2 * D_MODEL, C_MERGE // (2 * D_MODEL)),
                  const((D_INNER, D_MODEL)), const((N_HEADS * HEAD_DIM, D_MODEL)), const((D_MODEL, D_MODEL)),
                  const((1, D_MODEL)), const((D_MODEL, PEER_HEADS * PEER_DK))],
        out_specs=[rows(D_MODEL), rows(D_MODEL), rows(PEER_HEADS * PEER_DK)],
        out_shape=[out, out, jax.ShapeDtypeStruct((n, PEER_HEADS * PEER_DK), F32)],
        compiler_params=_cparams(("arbitrary",)),
        name="merge",
    )(x2d, y_ssm, o_attn, u2d, wps, wpa, wo, norm2_w.reshape(1, D_MODEL), wpq)


def _top_k_rows(s, k, payload=None):
    r = s.shape[0]
    ridx = lax.broadcasted_iota(jnp.int32, s.shape, 0)
    vals, picks = [], []
    for _ in range(k):
        m = jnp.max(s, axis=0, keepdims=True)
        idx = jnp.min(jnp.where(s == m, ridx, r), axis=0, keepdims=True)
        chosen = ridx == idx
        vals.append(m)
        picks.append(idx if payload is None else jnp.sum(jnp.where(chosen, payload, 0), axis=0, keepdims=True))
        s = jnp.where(chosen, -jnp.inf, s)
    return jnp.concatenate(vals, axis=0), jnp.concatenate(picks, axis=0)


def _peer_select_kernel(pq_ref, sk_ref, e_ref, g_ref):
    half = PEER_DK // 2
    es, gs = [], []
    for hd in range(PEER_HEADS):
        tops = []
        for cidx in range(2):
            col = (hd * 2 + cidx) * half
            q = pq_ref[:, col:col + half].astype(BF16)
            tops.append(_top_k_rows(_nt(sk_ref[hd, cidx], q), PEER_TOPK))
        (s1, i1), (s2, i2) = tops
        n_b = [PEER_TOPK // (a + 1) for a in range(PEER_TOPK)]
        pad = -sum(n_b) % 8
        tn = s1.shape[1]
        cand = jnp.concatenate([s1[a:a + 1] + s2[:n_b[a]] for a in range(PEER_TOPK)]
                               + [jnp.full((pad, tn), -jnp.inf, F32)], axis=0)
        expert = jnp.concatenate([i1[a:a + 1] * N_KEYS + i2[:n_b[a]] for a in range(PEER_TOPK)]
                                 + [jnp.zeros((pad, tn), jnp.int32)], axis=0)
        top_s, top_e = _top_k_rows(cand, PEER_TOPK, payload=expert)
        es.append(top_e)
        ex = jnp.exp(top_s - top_s[0:1])
        gs.append(ex / jnp.sum(ex, axis=0, keepdims=True))
    e_ref[...] = jnp.concatenate(es, axis=0).T
    g_ref[...] = jnp.concatenate(gs, axis=0).T


def _peer_select(pq, sub_keys_bf):
    n = pq.shape[0]
    tn = min(256, n)
    nk = PEER_HEADS * PEER_TOPK
    return pl.pallas_call(
        _peer_select_kernel,
        grid=(n // tn,),
        in_specs=[pl.BlockSpec((tn, PEER_HEADS * PEER_DK), lambda i: (i, 0)),
                  pl.BlockSpec((PEER_HEADS, 2, N_KEYS, PEER_DK // 2), lambda i: (0, 0, 0, 0))],
        out_specs=[pl.BlockSpec((tn, nk), lambda i: (i, 0)), pl.BlockSpec((tn, nk), lambda i: (i, 0))],
        out_shape=[jax.ShapeDtypeStruct((n, nk), jnp.int32), jax.ShapeDtypeStruct((n, nk), F32)],
        compiler_params=_cparams(("arbitrary",)),
        name="peer_select",
    )(pq, sub_keys_bf)


PEER_TOK = 8


FEAT_TILES = D_MODEL // LANES
EXPERT_ROWS = 2 * FEAT_TILES


def _peer_gather_kernel(e_ref, en_ref, g_ref, h_ref, x_ref, uv_hbm, o_ref, buf_ref, sem_ref, *, n):
    i = pl.program_id(0)
    nk = PEER_HEADS * PEER_TOPK
    rows = PEER_TOK * nk * EXPERT_ROWS

    def issue(idx_ref, slot):
        def tok_body(t, carry):
            for k in range(nk):
                src = pl.multiple_of(idx_ref[t, k] * EXPERT_ROWS, EXPERT_ROWS)
                dst = pl.multiple_of((t * nk + k) * EXPERT_ROWS, EXPERT_ROWS)
                pltpu.make_async_copy(uv_hbm.at[pl.ds(src, EXPERT_ROWS)],
                                      buf_ref.at[slot, pl.ds(dst, EXPERT_ROWS)], sem_ref.at[slot]).start()
            return carry
        lax.fori_loop(0, PEER_TOK, tok_body, 0)

    @pl.when(i == 0)
    def _():
        issue(e_ref, 0)

    @pl.when(i + 1 < n)
    def _():
        issue(en_ref, (i + 1) % 2)

    slot = i % 2
    pltpu.make_async_copy(uv_hbm.at[pl.ds(0, rows)], buf_ref.at[slot], sem_ref.at[slot]).wait()

    eye = (lax.broadcasted_iota(jnp.int32, (nk, nk), 0) == lax.broadcasted_iota(jnp.int32, (nk, nk), 1)).astype(F32)
    g_t = _nt(eye, g_ref[...], precision=HI)
    for t in range(PEER_TOK):
        tile_row = lambda s, t=t: buf_ref[slot, pl.ds(t * nk * EXPERT_ROWS + s, nk, stride=EXPERT_ROWS), :]
        prod = tile_row(0) * h_ref[t, 0:1, :]
        for s in range(1, FEAT_TILES):
            prod = prod + tile_row(s) * h_ref[t, s:s + 1, :]
        pre = jnp.sum(prod, axis=-1, keepdims=True)
        act = 0.5 * pre * (1.0 + lax.erf(pre * np.float32(math.sqrt(0.5))))
        coef = g_t[:, t:t + 1] * act
        out = jnp.concatenate([jnp.sum(coef * tile_row(FEAT_TILES + s), axis=0, keepdims=True)
                               for s in range(FEAT_TILES)], axis=0)
        o_ref[t] = x_ref[t] + out


def _peer_gather(e, g, h2, x2, uv):
    n = h2.shape[0]
    nk = PEER_HEADS * PEER_TOPK
    steps = n // PEER_TOK
    tiles = lambda: pl.BlockSpec((PEER_TOK, FEAT_TILES, LANES), lambda i: (i, 0, 0))
    smem = lambda imap: pl.BlockSpec((PEER_TOK, nk), imap, memory_space=pltpu.SMEM)
    return pl.pallas_call(
        functools.partial(_peer_gather_kernel, n=steps),
        grid=(steps,),
        in_specs=[smem(lambda i: (i, 0)), smem(lambda i: (jnp.minimum(i + 1, steps - 1), 0)),
                  pl.BlockSpec((PEER_TOK, nk), lambda i: (i, 0)), tiles(), tiles(),
                  pl.BlockSpec(memory_space=pl.ANY)],
        out_specs=tiles(),
        out_shape=jax.ShapeDtypeStruct((n, FEAT_TILES, LANES), F32),
        scratch_shapes=[pltpu.VMEM((2, PEER_TOK * nk * EXPERT_ROWS, LANES), F32), pltpu.SemaphoreType.DMA((2,))],
        compiler_params=_cparams(("arbitrary",)),
        name="peer_gather",
    )(e, e, g, h2.reshape(n, FEAT_TILES, LANES), x2.reshape(n, FEAT_TILES, LANES), uv).reshape(n, D_MODEL)


def _permute_w_in(w_in):
    sizes = (D_INNER, CONV_DIM, SSM_HEADS, N_HEADS * HEAD_DIM, 3 * ROW_W, 3 * N_HEADS, 2 * D_MODEL)
    cuts = np.cumsum((0,) + sizes)
    z, xbc, dt, q, kv, ng, mg = (w_in[:, cuts[i]:cuts[i + 1]] for i in range(7))
    pad = jnp.zeros((D_MODEL, LANES - SSM_HEADS - 3 * N_HEADS), w_in.dtype)
    return jnp.concatenate([xbc, z, mg, q, kv, dt, ng, pad], axis=1).astype(BF16)


def _pack_experts(expert_u, expert_v):
    n = expert_u.shape[0]
    tile = lambda a: a.reshape(n, FEAT_TILES, LANES)
    return jnp.concatenate([tile(expert_u), tile(expert_v)], axis=1).reshape(n * EXPERT_ROWS, LANES)


def _layer_weights(norm1_w, w_in, conv_w, conv_b, dt_bias, a_log, d_skip, ssm_norm_w, w_proj_ssm, q_norm_w,
                   k_norm_w, cmp_pos, w_cmp, w_proj_attn, w_out, norm2_w, w_pq, sub_keys, expert_u, expert_v):
    wbd, pos, kg = _compress_weights(w_cmp, cmp_pos, k_norm_w[0])
    return dict(norm1_w=norm1_w, w_in=_permute_w_in(w_in), conv_w=conv_w, conv_b=conv_b, dt_bias=dt_bias,
                a_log=a_log, d_skip=d_skip, ssm_norm_w=ssm_norm_w, wps=w_proj_ssm.astype(BF16),
                q_norm_w=q_norm_w, k_norm_w=k_norm_w, wbd=wbd, pos=pos, kg=kg, wpa=w_proj_attn.astype(BF16),
                wo=w_out.astype(BF16), norm2_w=norm2_w, wpq=w_pq.astype(BF16), sub_keys=sub_keys.astype(BF16),
                uv=_pack_experts(expert_u, expert_v))


def _token_tail(x2d, y_ssm, o_attn, u2d, w):
    x2, h2, pq = _merge(x2d, y_ssm, o_attn, u2d, w["wps"], w["wpa"], w["wo"], w["norm2_w"], w["wpq"])
    e, g = _peer_select(pq, w["sub_keys"])
    return _peer_gather(e, g, h2, x2, w["uv"])


def _layer_prompt(x, w, win_buf):
    b, t = x.shape[:2]
    assert t % (2 * Q_BLOCK) == 0 and t >= WINDOW + Q_BLOCK and t >= win_buf
    x2d = x.reshape(b * t, D_MODEL)
    u2d = _inproj(x2d, w["norm1_w"], w["w_in"])
    u3 = u2d.reshape(b, t, D_IN_P)
    y_ssm, h_last = _ssd(u3, jnp.zeros((b, D_INNER, D_STATE), F32), jnp.zeros((b, 8, CONV_DIM), F32),
                         w["conv_w"], w["conv_b"], w["dt_bias"], w["a_log"], w["d_skip"], w["ssm_norm_w"])
    cos_t, sin_t = _rope_tables(jnp.arange(t, dtype=jnp.int32))
    qn, qr, slc, win = _nsa_prep(u2d, cos_t, sin_t, w["q_norm_w"], w["k_norm_w"], t)
    cmp_rows = u3[:, :, C_KV:C_KV + ROW_W]
    kc, vc = _compress_prompt(cmp_rows, w["wbd"], w["pos"], w["kg"])
    hw = N_HEADS * HEAD_DIM
    o_attn = _nsa_prompt(qn.reshape(b, t, hw), qr.reshape(b, t, hw), u3, kc, vc,
                         slc.reshape(b, t, ROW_W), win.reshape(b, t, ROW_W))
    y = _token_tail(x2d, y_ssm.reshape(b * t, D_INNER), o_attn.reshape(b * t, hw), u2d, w)
    rows = lambda a: a.reshape(b, -1, 2, N_KV, HEAD_DIM)
    state = (rows(cmp_rows), rows(slc), rows(win.reshape(b, t, ROW_W)[:, t - win_buf:]),
             h_last.reshape(b, SSM_HEADS, SSM_HEADDIM, D_STATE), u3[:, t - (CONV_W - 1):, C_XBC:C_XBC + CONV_DIM])
    return y.reshape(b, t, D_MODEL), state


def _layer_sample(x, w, cache_cmp, cache_slc, page_table, win_state, h0, conv_state):
    b, t = x.shape[:2]
    assert t == 1
    past = page_table.shape[1] * PAGE_SIZE
    win_buf = win_state.shape[1]
    assert page_table.shape[1] % N_PARTS == 0 and win_buf >= 1
    x2d = x.reshape(b, D_MODEL)
    u2d = _inproj(x2d, w["norm1_w"], w["w_in"])
    u3 = u2d.reshape(b, 1, D_IN_P)
    prefix8 = jnp.pad(conv_state, ((0, 0), (8 - (CONV_W - 1), 0), (0, 0)))
    y_ssm, h_last = _ssd(u3, h0.reshape(b, D_INNER, D_STATE), prefix8, w["conv_w"], w["conv_b"], w["dt_bias"],
                         w["a_log"], w["d_skip"], w["ssm_norm_w"])
    cos_t, sin_t = _rope_tables(jnp.full((1,), past, jnp.int32))
    qn, qr, slc, win = _nsa_prep(u2d, cos_t, sin_t, w["q_norm_w"], w["k_norm_w"], 1)
    n_pool = cache_cmp.shape[0]
    page_t = lambda cache: jnp.transpose(cache, (0, 2, 3, 4, 1)).reshape(n_pool, ROW_W, PAGE_SIZE)
    o_attn = _nsa_sample_t(qn, qr, u2d, slc, win, win_state.reshape(b, win_buf, ROW_W), page_table,
                           page_t(cache_cmp), page_t(cache_slc), w["wbd"], w["pos"], w["kg"])
    y = _token_tail(x2d, y_ssm.reshape(b, D_INNER), o_attn, u2d, w)
    rows = lambda a: a.reshape(b, -1, 2, N_KV, HEAD_DIM)
    new_win = jnp.concatenate([win_state.reshape(b, win_buf, ROW_W), win.reshape(b, 1, ROW_W)], axis=1)[:, 1:]
    new_conv = jnp.concatenate([conv_state, u3[:, :, C_XBC:C_XBC + CONV_DIM]], axis=1)[:, 1:]
    state = (rows(u2d[:, C_KV:C_KV + ROW_W]), rows(slc), rows(new_win),
             h_last.reshape(b, SSM_HEADS, SSM_HEADDIM, D_STATE), new_conv)
    return y.reshape(b, 1, D_MODEL), state


def kernel(x_prompt, x_sample, cache_cmp_kv, cache_slc_kv, page_table, state_win_kv, state_ssm, state_conv,
           norm1_w, w_in, conv_w, conv_b, dt_bias, a_log, d_skip, ssm_norm_w, w_proj_ssm, q_norm_w, k_norm_w,
           cmp_pos, w_cmp, w_proj_attn, w_out, norm2_w, w_pq, sub_keys, expert_u, expert_v):
    depth = w_in.shape[0]
    win_buf = state_win_kv.shape[2]
    yp, ys = x_prompt, x_sample
    new_p, new_s = [], []
    for l in range(depth):
        w = _layer_weights(norm1_w[l], w_in[l], conv_w[l], conv_b[l], dt_bias[l], a_log[l], d_skip[l],
                           ssm_norm_w[l], w_proj_ssm[l], q_norm_w[l], k_norm_w[l], cmp_pos[l], w_cmp[l],
                           w_proj_attn[l], w_out[l], norm2_w[l], w_pq[l], sub_keys[l], expert_u[l], expert_v[l])
        yp, st_p = _layer_prompt(yp, w, win_buf)
        ys, st_s = _layer_sample(ys, w, cache_cmp_kv[l], cache_slc_kv[l], page_table, state_win_kv[l],
                                 state_ssm[l], state_conv[l])
        new_p.append(st_p)
        new_s.append(st_s)
    stack = lambda states, i: jnp.stack([s[i] for s in states], axis=0)
    return (yp, ys, stack(new_p, 0), stack(new_s, 0), stack(new_p, 1), stack(new_s, 1), stack(new_p, 2),
            stack(new_s, 2), stack(new_p, 3), stack(new_s, 3), stack(new_p, 4), stack(new_s, 4))
```

```python
import functools
import math

import jax
import jax.numpy as jnp
import numpy as np
from jax import lax
from jax.experimental import pallas as pl
from jax.experimental.pallas import tpu as pltpu

F32 = jnp.float32
BF16 = jnp.bfloat16
HI = lax.Precision.HIGHEST

D_MODEL = 1024
PAGE_SIZE = 128
D_INNER = 2048
SSM_HEADDIM = 64
SSM_HEADS = 32
SSM_GROUPS = 8
SSM_HPG = 4
D_STATE = 128
CONV_W = 4
CONV_DIM = 4096
SSD_CHUNK = 128
N_HEADS = 16
N_KV = 4
GQA = 4
HEAD_DIM = 64
CMP_STRIDE = 16
CMP_LEN = 32
SLC_BLOCK = 64
SUB_PER_SLC = 4
N_SEL = 16
WINDOW = 512
Q_BLOCK = 128
ROPE_THETA = 10000.0
ATTN_SCALE = HEAD_DIM ** -0.5
PEER_HEADS = 8
N_KEYS = 128
PEER_TOPK = 16
PEER_DK = 128
EPS = 1e-6
NEG = -1e9
FORCE_BONUS = 1e3

LANES = 128
KV_W = N_KV * HEAD_DIM
ROW_W = 2 * KV_W
ROW_TILES = ROW_W // LANES
C_XBC, C_Z, C_MERGE, C_Q, C_KV, C_SMALL = 0, 4096, 6144, 8192, 9216, 10752
D_IN_P = 10880
IN_TILE_N = 2176
VMEM_LIMIT = 56 * 1024 * 1024


def _cparams(sem, vmem=VMEM_LIMIT):
    return pltpu.CompilerParams(dimension_semantics=sem, vmem_limit_bytes=vmem)


def _nt(a, b, precision=None):
    return lax.dot_general(a, b, (((1,), (1,)), ((), ())), preferred_element_type=F32, precision=precision)


def _tn(a, b, precision=None):
    return lax.dot_general(a, b, (((0,), (0,)), ((), ())), preferred_element_type=F32, precision=precision)


def _mm(a, b, precision=None):
    return jnp.dot(a, b, preferred_element_type=F32, precision=precision)


def _sigmoid(x):
    return 1.0 / (1.0 + jnp.exp(-x))


def _silu(x):
    return x * _sigmoid(x)


def _softplus(x):
    return jnp.maximum(x, 0.0) + jnp.log1p(jnp.exp(-jnp.abs(x)))


def _inproj_kernel(x_ref, nw_ref, w_ref, o_ref):
    x = x_ref[...]
    h = x * lax.rsqrt(jnp.mean(x * x, axis=-1, keepdims=True) + EPS) * nw_ref[...]
    o_ref[...] = _mm(h.astype(BF16), w_ref[...])


def _inproj(x2d, norm_w, w_p):
    n = x2d.shape[0]
    tm = min(512, n)
    return pl.pallas_call(
        _inproj_kernel,
        grid=(D_IN_P // IN_TILE_N, n // tm),
        in_specs=[pl.BlockSpec((tm, D_MODEL), lambda j, i: (i, 0)),
                  pl.BlockSpec((1, D_MODEL), lambda j, i: (0, 0)),
                  pl.BlockSpec((D_MODEL, IN_TILE_N), lambda j, i: (0, j))],
        out_specs=pl.BlockSpec((tm, IN_TILE_N), lambda j, i: (i, j)),
        out_shape=jax.ShapeDtypeStruct((n, D_IN_P), F32),
        compiler_params=_cparams(("arbitrary", "arbitrary")),
        name="inproj",
    )(x2d, norm_w.reshape(1, D_MODEL), w_p)


def _ssd_kernel(xbc_ref, z_ref, sm_ref, h0_ref, pre_ref, cw_ref, cb_ref, dtb_ref, alog_ref, dsk_ref, nw_ref,
                y_ref, hl_ref, st_ref, xe_ref, *, q, qp):
    c = pl.program_id(1)

    @pl.when(c == 0)
    def _():
        st_ref[...] = h0_ref[0]
        xe_ref[0:8, :] = pre_ref[0]

    @pl.when(c > 0)
    def _():
        xe_ref[0:8, :] = xe_ref[qp:qp + 8, :]

    xe_ref[8:8 + q, :] = xbc_ref[0]
    if qp > q:
        xe_ref[8 + q:8 + qp, :] = jnp.zeros((qp - q, CONV_DIM), F32)

    conv = cb_ref[...] + xe_ref[8:8 + qp, :] * cw_ref[3:4, :]
    for s in range(1, CONV_W):
        conv = conv + xe_ref[8 - s:8 - s + qp, :] * cw_ref[3 - s:4 - s, :]
    act = _silu(conv)

    lane = lax.broadcasted_iota(jnp.int32, (1, LANES), 1)
    head_lane = lane < SSM_HEADS
    sm = sm_ref[0]
    if qp > q:
        sm = jnp.concatenate([sm, jnp.zeros((qp - q, LANES), F32)], axis=0)
    dt = _softplus(sm + dtb_ref[...])
    dt = jnp.where(head_lane, dt, 0.0)
    if qp > q:
        row = lax.broadcasted_iota(jnp.int32, (qp, 1), 0)
        dt = jnp.where(row < q, dt, 0.0)
    a = jnp.where(head_lane, -jnp.exp(alog_ref[...]), 0.0)
    da = dt * a
    ri = lax.broadcasted_iota(jnp.int32, (qp, qp), 0)
    ci = lax.broadcasted_iota(jnp.int32, (qp, qp), 1)
    causal = ci <= ri
    cs = _mm(causal.astype(F32), da, precision=HI)
    eye = (lax.broadcasted_iota(jnp.int32, (LANES, LANES), 0)
           == lax.broadcasted_iota(jnp.int32, (LANES, LANES), 1)).astype(F32)
    cs_t = _nt(eye, cs, precision=HI)
    cs_last = cs[qp - 1:qp, :]
    e_cs = jnp.exp(cs)
    e_end = jnp.exp(cs_last - cs)
    e_last = jnp.exp(cs_last)

    for g in range(SSM_GROUPS):
        bm = act[:, D_INNER + g * D_STATE:D_INNER + (g + 1) * D_STATE].astype(BF16)
        cm = act[:, D_INNER + SSM_GROUPS * D_STATE + g * D_STATE:
                 D_INNER + SSM_GROUPS * D_STATE + (g + 1) * D_STATE].astype(BF16)
        cbm = _nt(cm, bm)
        r0 = g * SSM_HPG * SSM_HEADDIM
        s_g = st_ref[r0:r0 + SSM_HPG * SSM_HEADDIM, :]
        y_off = _nt(cm, s_g.astype(BF16))
        ys, xds, decs = [], [], []
        for r in range(SSM_HPG):
            h = g * SSM_HPG + r
            xs = act[:, h * SSM_HEADDIM:(h + 1) * SSM_HEADDIM]
            col = cs[:, h:h + 1]
            rowv = cs_t[h:h + 1, :]
            lmat = jnp.exp(jnp.where(causal, col - rowv, -jnp.inf))
            xdt = xs * dt[:, h:h + 1]
            y_d = _mm((cbm * lmat).astype(BF16), xdt.astype(BF16))
            ys.append(y_d + y_off[:, r * SSM_HEADDIM:(r + 1) * SSM_HEADDIM] * e_cs[:, h:h + 1]
                      + xs * dsk_ref[:, h:h + 1])
            xds.append(xdt * e_end[:, h:h + 1])
            decs.append(jnp.broadcast_to(e_last[:, h:h + 1], (SSM_HEADDIM, 1)))
        xd = jnp.concatenate(xds, axis=1)
        new = _tn(xd.astype(BF16), bm)
        dec = jnp.concatenate(decs, axis=0)
        st_ref[r0:r0 + SSM_HPG * SSM_HEADDIM, :] = dec * s_g + new
        yg = jnp.concatenate(ys, axis=1)
        w = D_INNER // SSM_GROUPS
        zg = z_ref[0][:, g * w:(g + 1) * w]
        if qp > q:
            yg = yg[:q]
        yg = yg * _silu(zg)
        yg = yg * lax.rsqrt(jnp.mean(yg * yg, axis=-1, keepdims=True) + EPS) * nw_ref[:, g * w:(g + 1) * w]
        y_ref[0, :, g * w:(g + 1) * w] = yg

    @pl.when(c == pl.num_programs(1) - 1)
    def _():
        hl_ref[0] = st_ref[...]


def _pad_lanes(v, width=LANES):
    v = v.reshape(1, -1)
    return jnp.pad(v, ((0, 0), (0, width - v.shape[1])))


def _ssd(u3, h0, prefix8, conv_w, conv_b, dt_bias, a_log, d_skip, ssm_norm_w):
    b, t = u3.shape[:2]
    q = min(SSD_CHUNK, t)
    qp = max(q, 8)
    nc = t // q
    row = lambda shape: pl.BlockSpec(shape, lambda i, c: (0, 0))
    return pl.pallas_call(
        functools.partial(_ssd_kernel, q=q, qp=qp),
        grid=(b, nc),
        in_specs=[pl.BlockSpec((1, q, CONV_DIM), lambda i, c: (i, c, C_XBC // CONV_DIM)),
                  pl.BlockSpec((1, q, D_INNER), lambda i, c: (i, c, C_Z // D_INNER)),
                  pl.BlockSpec((1, q, LANES), lambda i, c: (i, c, C_SMALL // LANES)),
                  pl.BlockSpec((1, D_INNER, D_STATE), lambda i, c: (i, 0, 0)),
                  pl.BlockSpec((1, 8, CONV_DIM), lambda i, c: (i, 0, 0)),
                  row((CONV_W, CONV_DIM)), row((1, CONV_DIM)), row((1, LANES)), row((1, LANES)), row((1, LANES)),
                  row((1, D_INNER))],
        out_specs=[pl.BlockSpec((1, q, D_INNER), lambda i, c: (i, c, 0)),
                   pl.BlockSpec((1, D_INNER, D_STATE), lambda i, c: (i, 0, 0))],
        out_shape=[jax.ShapeDtypeStruct((b, t, D_INNER), F32),
                   jax.ShapeDtypeStruct((b, D_INNER, D_STATE), F32)],
        scratch_shapes=[pltpu.VMEM((D_INNER, D_STATE), F32), pltpu.VMEM((qp + 8, CONV_DIM), F32)],
        compiler_params=_cparams(("arbitrary", "arbitrary")),
        name="ssd",
    )(u3, u3, u3, h0, prefix8, conv_w, conv_b.reshape(1, CONV_DIM), _pad_lanes(dt_bias), _pad_lanes(a_log),
      _pad_lanes(d_skip), ssm_norm_w.reshape(1, D_INNER))


def _head_sumsq(x, bd):
    sq = x * x
    hi = sq.astype(BF16)
    lo = (sq - hi.astype(F32)).astype(BF16)
    return _mm(hi, bd) + _mm(lo, bd)


def _rope_tiles(x, cos, sin_signed):
    n_tiles = x.shape[1] // LANES
    lane = lax.broadcasted_iota(jnp.int32, (1, LANES), 1)
    first_half = (lane % HEAD_DIM) < (HEAD_DIM // 2)
    outs = []
    for i in range(n_tiles):
        xt = x[:, i * LANES:(i + 1) * LANES]
        swapped = jnp.where(first_half, pltpu.roll(xt, LANES - HEAD_DIM // 2, 1), pltpu.roll(xt, HEAD_DIM // 2, 1))
        outs.append(xt * cos + swapped * sin_signed)
    return jnp.concatenate(outs, axis=1)


def _nsa_prep_kernel(q_ref, kv_ref, cos_ref, sin_ref, bd_ref, qw_ref, kw_ref, qn_ref, qr_ref, slc_ref, win_ref):
    cos = cos_ref[...]
    sin = sin_ref[...]
    bd = bd_ref[...]
    q = q_ref[...]
    qn = q * lax.rsqrt(_head_sumsq(q, bd) * (1.0 / HEAD_DIM) + EPS) * qw_ref[...]
    qn_ref[...] = qn
    qr_ref[...] = _rope_tiles(qn, cos, sin)
    for br, o_ref in ((1, slc_ref), (2, win_ref)):
        k = kv_ref[:, br * ROW_W:br * ROW_W + KV_W]
        kn = k * lax.rsqrt(_head_sumsq(k, bd[:KV_W, :KV_W]) * (1.0 / HEAD_DIM) + EPS) * kw_ref[br:br + 1, :]
        o_ref[:, :KV_W] = _rope_tiles(kn, cos, sin)
        o_ref[:, KV_W:] = kv_ref[:, br * ROW_W + KV_W:(br + 1) * ROW_W]


def _nsa_prep(u2d, cos_t, sin_t, q_norm_w, k_norm_w, t):
    n = u2d.shape[0]
    tm = min(512, n, t) if t > 1 else n
    tab_blocks = max(t // tm, 1)
    if t == 1:
        cos_t = jnp.broadcast_to(cos_t, (tm, LANES))
        sin_t = jnp.broadcast_to(sin_t, (tm, LANES))
    head_id = np.arange(N_HEADS * HEAD_DIM) // HEAD_DIM
    bd = jnp.asarray(head_id[:, None] == head_id[None, :], BF16)
    qw = jnp.tile(q_norm_w, N_HEADS).reshape(1, -1)
    kw = jnp.tile(k_norm_w, (1, N_KV))
    tab = lambda: pl.BlockSpec((tm, LANES), lambda i: (i % tab_blocks, 0))
    return pl.pallas_call(
        _nsa_prep_kernel,
        grid=(n // tm,),
        in_specs=[pl.BlockSpec((tm, N_HEADS * HEAD_DIM), lambda i: (i, C_Q // (N_HEADS * HEAD_DIM))),
                  pl.BlockSpec((tm, 3 * ROW_W), lambda i: (i, C_KV // (3 * ROW_W))),
                  tab(), tab(),
                  pl.BlockSpec((N_HEADS * HEAD_DIM, N_HEADS * HEAD_DIM), lambda i: (0, 0)),
                  pl.BlockSpec((1, N_HEADS * HEAD_DIM), lambda i: (0, 0)),
                  pl.BlockSpec((3, KV_W), lambda i: (0, 0))],
        out_specs=[pl.BlockSpec((tm, N_HEADS * HEAD_DIM), lambda i: (i, 0)),
                   pl.BlockSpec((tm, N_HEADS * HEAD_DIM), lambda i: (i, 0)),
                   pl.BlockSpec((tm, ROW_W), lambda i: (i, 0)),
                   pl.BlockSpec((tm, ROW_W), lambda i: (i, 0))],
        out_shape=[jax.ShapeDtypeStruct((n, N_HEADS * HEAD_DIM), F32),
                   jax.ShapeDtypeStruct((n, N_HEADS * HEAD_DIM), F32),
                   jax.ShapeDtypeStruct((n, ROW_W), F32),
                   jax.ShapeDtypeStruct((n, ROW_W), F32)],
        compiler_params=_cparams(("arbitrary",)),
        name="nsa_prep",
    )(u2d, u2d, cos_t, sin_t, bd, qw, kw)


def _rope_tables(pos):
    half = HEAD_DIM // 2
    inv = ROPE_THETA ** (-jnp.arange(half, dtype=F32) / half)
    ang = pos.astype(F32)[:, None] * inv[None, :]
    cos = jnp.cos(ang)
    sin = jnp.sin(ang)
    reps = LANES // HEAD_DIM
    return (jnp.tile(jnp.concatenate([cos, cos], axis=1), (1, reps)),
            jnp.tile(jnp.concatenate([-sin, sin], axis=1), (1, reps)))


def _sub_block_proj(load_rows, wbd_ref, pos_ref):
    acc = [[None, None], [None, None]]
    per = KV_W // LANES
    for l in range(CMP_STRIDE):
        for cidx in range(2):
            rows = jnp.concatenate([load_rows(l, cidx * per + j) for j in range(per)], axis=1)
            for half in range(2):
                x = rows + pos_ref[half * CMP_STRIDE + l, cidx:cidx + 1, :]
                part = _mm(x.astype(BF16), wbd_ref[half * CMP_STRIDE + l, cidx])
                acc[half][cidx] = part if acc[half][cidx] is None else acc[half][cidx] + part
    return acc


def _compress_finish(first_k, first_v, second_k, second_v, kg):
    n_sub = first_k.shape[0]
    k = first_k + pltpu.roll(second_k, n_sub - 1, 0)
    v = first_v + pltpu.roll(second_v, n_sub - 1, 0)
    parts = []
    for hd in range(N_KV):
        kh = k[:, hd * HEAD_DIM:(hd + 1) * HEAD_DIM]
        parts.append(kh * lax.rsqrt(jnp.mean(kh * kh, axis=-1, keepdims=True) + EPS))
    return jnp.concatenate(parts, axis=1) * kg, v


def _compress_kernel(rows_ref, wbd_ref, pos_ref, kg_ref, kc_ref, vc_ref, *, n_sub):
    acc = _sub_block_proj(lambda l, j: rows_ref[0, pl.ds(l * ROW_TILES + j, n_sub, stride=CMP_STRIDE * ROW_TILES), :],
                          wbd_ref, pos_ref)
    kc, vc = _compress_finish(acc[0][0], acc[0][1], acc[1][0], acc[1][1], kg_ref[...])
    kc_ref[0] = kc
    vc_ref[0] = vc


def _compress_weights(w_cmp, cmp_pos, k_gain):
    eye = jnp.eye(N_KV, dtype=F32)
    wbd = jnp.einsum("hg,lcde->lchdge", eye, w_cmp).reshape(CMP_LEN, 2, KV_W, KV_W).astype(BF16)
    pos = jnp.tile(cmp_pos, (1, 1, N_KV))
    kg = jnp.tile(k_gain, N_KV).reshape(1, KV_W)
    return wbd, pos, kg


def _compress_prompt(cmp_rows, wbd, pos, kg):
    b, t = cmp_rows.shape[:2]
    cmp_rows = cmp_rows.reshape(b, t * ROW_TILES, LANES)
    n_sub = t // CMP_STRIDE
    const = lambda shape: pl.BlockSpec(shape, lambda i: (0,) * len(shape))
    return pl.pallas_call(
        functools.partial(_compress_kernel, n_sub=n_sub),
        grid=(b,),
        in_specs=[pl.BlockSpec((1, t * ROW_TILES, LANES), lambda i: (i, 0, 0)),
                  const((CMP_LEN, 2, KV_W, KV_W)), const((CMP_LEN, 2, KV_W)), const((1, KV_W))],
        out_specs=[pl.BlockSpec((1, n_sub, KV_W), lambda i: (i, 0, 0)),
                   pl.BlockSpec((1, n_sub, KV_W), lambda i: (i, 0, 0))],
        out_shape=[jax.ShapeDtypeStruct((b, n_sub, KV_W), F32), jax.ShapeDtypeStruct((b, n_sub, KV_W), F32)],
        compiler_params=_cparams(("arbitrary",)),
        name="compress_prompt",
    )(cmp_rows, wbd, pos, kg)


def _softmax_rows(s, mask):
    s = jnp.where(mask, s, NEG)
    m = jnp.max(s, axis=-1, keepdims=True)
    e = jnp.exp(s - m)
    p = e / jnp.sum(e, axis=-1, keepdims=True)
    return jnp.where(mask, p, 0.0)


def _stack_heads(x, kv):
    return jnp.concatenate([x[:, (kv * GQA + g) * HEAD_DIM:(kv * GQA + g + 1) * HEAD_DIM] for g in range(GQA)],
                           axis=0)


def _block_scores_t(p, tq, n_cmp, n_blk, sel_mat_t):
    imp = p[0:tq]
    for g in range(1, GQA):
        imp = imp + p[g * tq:(g + 1) * tq]
    return _nt(sel_mat_t, imp, precision=HI)


def _top_rank_mask_t(score_t, n_blk):
    jidx = lax.broadcasted_iota(jnp.int32, score_t.shape, 0)
    cnt = jnp.zeros(score_t.shape, F32)
    for j in range(n_blk):
        rowv = score_t[j:j + 1, :]
        beats = (rowv > score_t) | ((rowv == score_t) & (j < jidx))
        cnt = cnt + beats.astype(F32)
    return cnt < float(N_SEL)


def _nsa_prompt_kernel(qn_ref, qr_ref, sm_ref, kc_ref, vc_ref, slc_ref, win_ref, selm_ref, exp_ref, o_ref,
                       selk_ref, *, t, n_cmp, n_blk):
    tq = Q_BLOCK
    n = pl.program_id(1)
    start = n * tq
    qn = qn_ref[0]
    qr = qr_ref[0]
    gates = _sigmoid(sm_ref[0])
    q_pos1 = start + lax.broadcasted_iota(jnp.int32, (tq, 1), 0)
    q_pos = jnp.concatenate([q_pos1] * GQA, axis=0)
    ci = lax.broadcasted_iota(jnp.int32, (1, n_cmp), 1)
    c_mask = (ci * CMP_STRIDE + (CMP_LEN - 1)) <= q_pos
    jcol = lax.broadcasted_iota(jnp.int32, (n_blk, 1), 0)
    qrow = start + lax.broadcasted_iota(jnp.int32, (1, tq), 1)
    cur = qrow // SLC_BLOCK
    forced_t = (jcol == 0) | (jcol == cur) | (jcol == cur - 1)
    valid_t = jcol * SLC_BLOCK <= qrow
    w_lo = jnp.maximum(start - WINDOW, 0)
    w_len = WINDOW + tq
    kp_w = w_lo + lax.broadcasted_iota(jnp.int32, (1, w_len), 1)
    w_bias = jnp.where((kp_w <= q_pos1) & (kp_w > q_pos1 - WINDOW), 0.0, NEG)
    tk = 2 * tq
    n_tiles = (n + 2) // 2
    kp_all = lax.broadcasted_iota(jnp.int32, (1, t), 1)
    eye_q = (lax.broadcasted_iota(jnp.int32, (tq, tq), 0)
             == lax.broadcasted_iota(jnp.int32, (tq, tq), 1)).astype(BF16)

    outs = []
    for kv in range(N_KV):
        qn_s = _stack_heads(qn, kv).astype(BF16)
        qr_s = (_stack_heads(qr, kv) * ATTN_SCALE).astype(BF16)
        lanes = slice(kv * HEAD_DIM, (kv + 1) * HEAD_DIM)
        vlanes = slice(KV_W + kv * HEAD_DIM, KV_W + (kv + 1) * HEAD_DIM)
        kc = kc_ref[0][:, lanes].astype(BF16)
        vc = vc_ref[0][:, lanes].astype(BF16)
        p = _softmax_rows(_nt(qn_s, kc) * ATTN_SCALE, c_mask)
        o_cmp = _mm(p.astype(BF16), vc)
        score_t = _block_scores_t(p, tq, n_cmp, n_blk, selm_ref[...])
        score_t = jnp.where(forced_t, score_t + FORCE_BONUS, score_t)
        score_t = jnp.where(valid_t, score_t, NEG)
        sel_t = _top_rank_mask_t(score_t, n_blk) & valid_t
        sel = _nt(eye_q, jnp.where(sel_t, 1.0, 0.0).astype(BF16))
        sel_keys = _mm(sel.astype(BF16), exp_ref[...])
        selk_ref[...] = jnp.where((sel_keys > 0.5) & (kp_all <= q_pos1), 0.0, NEG)

        def tile_body(i, carry, lanes=lanes, vlanes=vlanes, qr_s=qr_s):
            m, l, acc = carry
            k0 = pl.multiple_of(i * tk, tk)
            kt = slc_ref[0, pl.ds(k0, tk), lanes].astype(BF16)
            vt = slc_ref[0, pl.ds(k0, tk), vlanes].astype(BF16)
            bias = selk_ref[:, pl.ds(k0, tk)]
            s = _nt(qr_s, kt) + jnp.concatenate([bias] * GQA, axis=0)
            m_new = jnp.maximum(m, jnp.max(s, axis=-1, keepdims=True))
            alpha = jnp.exp(m - m_new)
            e = jnp.exp(s - m_new)
            return (m_new, alpha * l + jnp.sum(e, axis=-1, keepdims=True),
                    alpha * acc + _mm(e.astype(BF16), vt))

        m0 = jnp.full((GQA * tq, 1), NEG, F32)
        l0 = jnp.zeros((GQA * tq, 1), F32)
        a0 = jnp.zeros((GQA * tq, HEAD_DIM), F32)
        _, l_s, acc_s = lax.fori_loop(0, n_tiles, tile_body, (m0, l0, a0))
        o_slc = acc_s / l_s
        kw = win_ref[0, pl.ds(pl.multiple_of(w_lo, tq), w_len), lanes].astype(BF16)
        vw = win_ref[0, pl.ds(pl.multiple_of(w_lo, tq), w_len), vlanes].astype(BF16)
        sw = _nt(qr_s, kw) + jnp.concatenate([w_bias] * GQA, axis=0)
        ew = jnp.exp(sw - jnp.max(sw, axis=-1, keepdims=True))
        pw = ew / jnp.sum(ew, axis=-1, keepdims=True)
        o_win = _mm(pw.astype(BF16), vw)
        for g in range(GQA):
            hd = kv * GQA + g
            rows = slice(g * tq, (g + 1) * tq)
            g0 = gates[:, SSM_HEADS + hd:SSM_HEADS + hd + 1]
            g1 = gates[:, SSM_HEADS + N_HEADS + hd:SSM_HEADS + N_HEADS + hd + 1]
            g2 = gates[:, SSM_HEADS + 2 * N_HEADS + hd:SSM_HEADS + 2 * N_HEADS + hd + 1]
            outs.append(g0 * o_cmp[rows] + g1 * o_slc[rows] + g2 * o_win[rows])
    o_ref[0] = jnp.concatenate(outs, axis=1)


def _selection_matrices(n_cmp, n_blk, n_keys):
    i = np.arange(n_cmp)[None, :]
    j = np.arange(n_blk)[:, None]
    sel = ((i >= SUB_PER_SLC * j - 1) & (i <= SUB_PER_SLC * j + SUB_PER_SLC - 1)).astype(np.float32)
    key = np.arange(n_keys)[None, :]
    expand = (key // SLC_BLOCK == j).astype(np.float32)
    return jnp.asarray(sel), jnp.asarray(expand, BF16)


def _nsa_prompt(qn3, qr3, u3, kc, vc, slc3, win3):
    b, t = qn3.shape[:2]
    n_cmp = t // CMP_STRIDE
    n_blk = t // SLC_BLOCK
    selm, expand = _selection_matrices(n_cmp, n_blk, t)
    selm = selm.at[:, n_cmp - 1].set(0.0)
    hw = N_HEADS * HEAD_DIM
    full = lambda w: pl.BlockSpec((1, t, w), lambda i, n: (i, 0, 0))
    return pl.pallas_call(
        functools.partial(_nsa_prompt_kernel, t=t, n_cmp=n_cmp, n_blk=n_blk),
        grid=(b, t // Q_BLOCK),
        in_specs=[pl.BlockSpec((1, Q_BLOCK, hw), lambda i, n: (i, n, 0)),
                  pl.BlockSpec((1, Q_BLOCK, hw), lambda i, n: (i, n, 0)),
                  pl.BlockSpec((1, Q_BLOCK, LANES), lambda i, n: (i, n, C_SMALL // LANES)),
                  pl.BlockSpec((1, n_cmp, KV_W), lambda i, n: (i, 0, 0)),
                  pl.BlockSpec((1, n_cmp, KV_W), lambda i, n: (i, 0, 0)),
                  full(ROW_W), full(ROW_W),
                  pl.BlockSpec((n_blk, n_cmp), lambda i, n: (0, 0)),
                  pl.BlockSpec((n_blk, t), lambda i, n: (0, 0))],
        out_specs=pl.BlockSpec((1, Q_BLOCK, hw), lambda i, n: (i, n, 0)),
        out_shape=jax.ShapeDtypeStruct((b, t, hw), F32),
        scratch_shapes=[pltpu.VMEM((Q_BLOCK, t), F32)],
        compiler_params=_cparams(("arbitrary", "arbitrary")),
        name="nsa_prompt",
    )(qn3, qr3, u3, kc, vc, slc3, win3, selm, expand)


N_PARTS = 2


def _head_block_diag(x_ref, scale):
    q16 = jnp.concatenate([x_ref[0, :, hd * HEAD_DIM:(hd + 1) * HEAD_DIM] for hd in range(N_HEADS)], axis=0) * scale
    kv_of_row = lax.broadcasted_iota(jnp.int32, (N_HEADS, HEAD_DIM), 0) // GQA
    return jnp.concatenate([jnp.where(kv_of_row == kv, q16, 0.0) for kv in range(N_KV)], axis=1)


def _nsa_sample_t_kernel(pt_ref, qn_ref, qr_ref, sm_ref, slcn_ref, winn_ref, wins_ref, wbd_ref, pos_ref, kg_ref,
                         selm_ref, cmp_hbm, slc_hbm, o_ref,
                         buf_ref, sem_ref, rows_ref, fs_ref, sel_ref, ocmp_ref, m_ref, l_ref, acc_ref,
                         *, past, ppp, n_blk_pad, batch):
    b = pl.program_id(0)
    s = pl.program_id(1)
    n_steps = 2 * N_PARTS
    c = b * n_steps + s
    total = batch * n_steps
    rp = ppp * PAGE_SIZE
    nsp = rp // CMP_STRIDE
    n_sub = N_PARTS * nsp
    win_buf = wins_ref.shape[1]
    cur = past // SLC_BLOCK

    def page_copy(hbm, bb, part, pg, slot):
        return pltpu.make_async_copy(hbm.at[pt_ref[bb, part * ppp + pg]],
                                     buf_ref.at[slot, pl.ds(pg * ROW_W, ROW_W)], sem_ref.at[slot])

    def start_chunk(cc, slot):
        bb = cc // n_steps
        ss = cc % n_steps

        @pl.when(ss < N_PARTS)
        def _():
            for pg in range(ppp):
                page_copy(cmp_hbm, bb, ss, pg, slot).start()

        @pl.when(ss >= N_PARTS)
        def _():
            for pg in range(ppp):
                page_copy(slc_hbm, bb, ss - N_PARTS, pg, slot).start()

    @pl.when(c == 0)
    def _():
        start_chunk(c, 0)

    @pl.when(c + 1 < total)
    def _():
        start_chunk(c + 1, (c + 1) % 2)

    slot = c % 2
    for pg in range(ppp):
        page_copy(cmp_hbm, b, 0, pg, slot).wait()

    @pl.when(s < N_PARTS)
    def _():
        for pg in range(ppp):
            for j in range(ROW_TILES):
                tile = buf_ref[slot, pg * ROW_W + j * LANES:pg * ROW_W + (j + 1) * LANES, :]
                rows_ref[j, pg * PAGE_SIZE:(pg + 1) * PAGE_SIZE, :] = tile.T
        acc = _sub_block_proj(lambda l, j: rows_ref[j, pl.ds(l, nsp, stride=CMP_STRIDE), :], wbd_ref, pos_ref)
        row0 = pl.multiple_of(s * nsp, nsp)
        fs_ref[0, pl.ds(row0, nsp), :] = acc[0][0]
        fs_ref[1, pl.ds(row0, nsp), :] = acc[0][1]
        fs_ref[2, pl.ds(row0, nsp), :] = acc[1][0]
        fs_ref[3, pl.ds(row0, nsp), :] = acc[1][1]

    @pl.when(s == N_PARTS - 1)
    def _():
        kc, vc = _compress_finish(fs_ref[0], fs_ref[1], fs_ref[2], fs_ref[3], kg_ref[...])
        ci = lax.broadcasted_iota(jnp.int32, (1, n_sub), 1)
        c_mask = (ci * CMP_STRIDE + (CMP_LEN - 1)) <= past
        qn_bd = _head_block_diag(qn_ref, 1.0).astype(BF16)
        p = _softmax_rows(_nt(qn_bd, kc.astype(BF16)) * ATTN_SCALE, c_mask)
        ocmp_ref[...] = _mm(p.astype(BF16), vc.astype(BF16))
        imps = [p[kv * GQA:kv * GQA + 1] + p[kv * GQA + 1:kv * GQA + 2] + p[kv * GQA + 2:kv * GQA + 3]
                + p[kv * GQA + 3:kv * GQA + 4] for kv in range(N_KV)]
        imp = jnp.concatenate(imps + [jnp.zeros((8 - N_KV, n_sub), F32)], axis=0)
        score = _mm(imp, selm_ref[...], precision=HI)
        jrow = lax.broadcasted_iota(jnp.int32, (1, n_blk_pad), 1)
        forced = (jrow == 0) | (jrow == cur) | (jrow == cur - 1)
        valid = jrow * SLC_BLOCK <= past
        score = jnp.where(forced, score + FORCE_BONUS, score)
        score = jnp.where(valid, score, NEG)
        eye = (lax.broadcasted_iota(jnp.int32, (n_blk_pad, n_blk_pad), 0)
               == lax.broadcasted_iota(jnp.int32, (n_blk_pad, n_blk_pad), 1)).astype(F32)
        score_c = _nt(eye, score, precision=HI)
        jc = lax.broadcasted_iota(jnp.int32, (n_blk_pad, n_blk_pad), 0)
        jr = lax.broadcasted_iota(jnp.int32, (n_blk_pad, n_blk_pad), 1)
        sels = []
        for kv in range(N_KV):
            colv = score_c[:, kv:kv + 1]
            rowv = score[kv:kv + 1, :]
            beats = (colv > rowv) | ((colv == rowv) & (jc < jr))
            rank = jnp.sum(beats.astype(F32), axis=0, keepdims=True)
            sel = jnp.where((rank < float(N_SEL)) & valid, 1.0, 0.0)
            sels.extend([sel] * GQA)
        sel_ref[...] = jnp.concatenate(sels, axis=0)

    @pl.when(s >= N_PARTS)
    def _():
        part = s - N_PARTS

        @pl.when(part == 0)
        def _():
            m_ref[...] = jnp.full(m_ref.shape, NEG, F32)
            l_ref[...] = jnp.zeros(l_ref.shape, F32)
            acc_ref[...] = jnp.zeros(acc_ref.shape, F32)

        key = part * rp + lax.broadcasted_iota(jnp.int32, (n_blk_pad, rp), 1)
        blk = lax.broadcasted_iota(jnp.int32, (n_blk_pad, rp), 0)
        expand = jnp.where(key // SLC_BLOCK == blk, 1.0, 0.0).astype(BF16)
        ok = _mm(sel_ref[...].astype(BF16), expand) > 0.5
        qr_bd = _head_block_diag(qr_ref, ATTN_SCALE).astype(BF16)
        sc = jnp.concatenate([_mm(qr_bd, buf_ref[slot, pg * ROW_W:pg * ROW_W + KV_W, :].astype(BF16))
                              for pg in range(ppp)], axis=1)
        sc = jnp.where(ok, sc, NEG)
        m_old = m_ref[...]
        m_new = jnp.maximum(m_old, jnp.max(sc, axis=-1, keepdims=True))
        alpha = jnp.exp(m_old - m_new)
        e = jnp.where(ok, jnp.exp(sc - m_new), 0.0)
        m_ref[...] = m_new
        l_ref[...] = alpha * l_ref[...] + jnp.sum(e, axis=-1, keepdims=True)
        eb = e.astype(BF16)
        pv = None
        for pg in range(ppp):
            part_pv = _nt(eb[:, pg * PAGE_SIZE:(pg + 1) * PAGE_SIZE],
                          buf_ref[slot, pg * ROW_W + KV_W:(pg + 1) * ROW_W, :].astype(BF16))
            pv = part_pv if pv is None else pv + part_pv
        acc_ref[...] = alpha * acc_ref[...] + pv

    @pl.when(s == n_steps - 1)
    def _():
        qr_bd = _head_block_diag(qr_ref, ATTN_SCALE).astype(BF16).astype(F32)
        k_new = slcn_ref[0, :, :KV_W].astype(BF16).astype(F32)
        v_new = slcn_ref[0, :, KV_W:].astype(BF16).astype(F32)
        ok_new = sel_ref[:, cur:cur + 1] > 0.5
        s_new = jnp.where(ok_new, jnp.sum(qr_bd * k_new, axis=-1, keepdims=True), NEG)
        m_old = m_ref[...]
        m_new = jnp.maximum(m_old, s_new)
        alpha = jnp.exp(m_old - m_new)
        e_new = jnp.where(ok_new, jnp.exp(s_new - m_new), 0.0)
        l_s = alpha * l_ref[...] + e_new
        o_slc = (alpha * acc_ref[...] + e_new.astype(BF16).astype(F32) * v_new) / l_s
        kpos = past - win_buf + lax.broadcasted_iota(jnp.int32, (1, win_buf), 1)
        w_mask = (kpos <= past) & (kpos > past - WINDOW) & (kpos >= 0)
        sw = jnp.where(w_mask, _nt(qr_bd.astype(BF16), wins_ref[0, :, :KV_W].astype(BF16)), NEG)
        kw_new = winn_ref[0, :, :KV_W].astype(BF16).astype(F32)
        vw_new = winn_ref[0, :, KV_W:].astype(BF16).astype(F32)
        sw_new = jnp.sum(qr_bd * kw_new, axis=-1, keepdims=True)
        mw = jnp.maximum(jnp.max(sw, axis=-1, keepdims=True), sw_new)
        ew = jnp.where(w_mask, jnp.exp(sw - mw), 0.0)
        ew_new = jnp.exp(sw_new - mw)
        lw = jnp.sum(ew, axis=-1, keepdims=True) + ew_new
        o_win = (_mm((ew / lw).astype(BF16), wins_ref[0, :, KV_W:].astype(BF16))
                 + (ew_new / lw).astype(BF16).astype(F32) * vw_new)
        gates = _sigmoid(sm_ref[0])
        hrow = lax.broadcasted_iota(jnp.int32, (N_HEADS, LANES), 0)
        hlane = lax.broadcasted_iota(jnp.int32, (N_HEADS, LANES), 1)
        gcol = [jnp.sum(jnp.where(hlane == SSM_HEADS + br * N_HEADS + hrow, gates, 0.0), axis=-1, keepdims=True)
                for br in range(3)]
        mixed = gcol[0] * ocmp_ref[...] + gcol[1] * o_slc + gcol[2] * o_win
        o_ref[0] = jnp.concatenate(
            [mixed[hd:hd + 1, (hd // GQA) * HEAD_DIM:(hd // GQA + 1) * HEAD_DIM] for hd in range(N_HEADS)], axis=1)


def _nsa_sample_t(qn, qr, u2d, slc_new, win_new, win_state, page_table, cache_cmp_t, cache_slc_t, wbd, pos, kg):
    b = qn.shape[0]
    n_pages = page_table.shape[1]
    past = n_pages * PAGE_SIZE
    ppp = n_pages // N_PARTS
    n_sub = past // CMP_STRIDE
    n_blk = past // SLC_BLOCK + 1
    n_blk_pad = -(-n_blk // LANES) * LANES
    selm, _ = _selection_matrices(n_sub, n_blk_pad, 1)
    selm = selm.at[:, n_sub - 1].set(0.0).T
    hw = N_HEADS * HEAD_DIM
    win_buf = win_state.shape[1]
    n_steps = 2 * N_PARTS
    rp = ppp * PAGE_SIZE
    per_b = lambda w: pl.BlockSpec((1, 1, w), lambda i, s, pt: (i, 0, 0))
    const = lambda shape: pl.BlockSpec(shape, lambda i, s, pt: (0,) * len(shape))
    grid_spec = pltpu.PrefetchScalarGridSpec(
        num_scalar_prefetch=1,
        grid=(b, n_steps),
        in_specs=[per_b(hw), per_b(hw),
                  pl.BlockSpec((1, 1, LANES), lambda i, s, pt: (i, 0, C_SMALL // LANES)),
                  per_b(ROW_W), per_b(ROW_W),
                  pl.BlockSpec((1, win_buf, ROW_W), lambda i, s, pt: (i, 0, 0)),
                  const((CMP_LEN, 2, KV_W, KV_W)), const((CMP_LEN, 2, KV_W)), const((1, KV_W)),
                  const((n_sub, n_blk_pad)),
                  pl.BlockSpec(memory_space=pl.ANY), pl.BlockSpec(memory_space=pl.ANY)],
        out_specs=per_b(hw),
        scratch_shapes=[pltpu.VMEM((2, ppp * ROW_W, LANES), F32), pltpu.SemaphoreType.DMA((2,)),
                        pltpu.VMEM((ROW_TILES, rp, LANES), F32),
                        pltpu.VMEM((4, n_sub, KV_W), F32), pltpu.VMEM((N_HEADS, n_blk_pad), F32),
                        pltpu.VMEM((N_HEADS, KV_W), F32), pltpu.VMEM((N_HEADS, 1), F32),
                        pltpu.VMEM((N_HEADS, 1), F32), pltpu.VMEM((N_HEADS, KV_W), F32)])
    return pl.pallas_call(
        functools.partial(_nsa_sample_t_kernel, past=past, ppp=ppp, n_blk_pad=n_blk_pad, batch=b),
        grid_spec=grid_spec,
        out_shape=jax.ShapeDtypeStruct((b, 1, hw), F32),
        compiler_params=_cparams(("arbitrary", "arbitrary")),
        name="nsa_sample",
    )(page_table, qn.reshape(b, 1, hw), qr.reshape(b, 1, hw), u2d.reshape(b, 1, D_IN_P),
      slc_new.reshape(b, 1, ROW_W), win_new.reshape(b, 1, ROW_W), win_state, wbd, pos, kg, selm,
      cache_cmp_t, cache_slc_t).reshape(b, hw)


def _merge_kernel(x_ref, y_ref, o_ref, mg_ref, wps_ref, wpa_ref, wo_ref, n2_ref, wpq_ref, x2_ref, h2_ref, pq_ref):
    mg = _sigmoid(mg_ref[...])
    mixed = (mg[:, :D_MODEL] * _mm(y_ref[...].astype(BF16), wps_ref[...])
             + mg[:, D_MODEL:] * _mm(o_ref[...].astype(BF16), wpa_ref[...]))
    x2 = x_ref[...] + _mm(mixed.astype(BF16), wo_ref[...])
    x2_ref[...] = x2
    h2 = x2 * lax.rsqrt(jnp.mean(x2 * x2, axis=-1, keepdims=True) + EPS) * n2_ref[...]
    h2_ref[...] = h2
    pq_ref[...] = _mm(h2.astype(BF16), wpq_ref[...])


def _merge(x2d, y_ssm, o_attn, u2d, wps, wpa, wo, norm2_w, wpq):
    n = x2d.shape[0]
    tm = min(256, n)
    rows = lambda w, cb=0: pl.BlockSpec((tm, w), lambda i: (i, cb))
    const = lambda shape: pl.BlockSpec(shape, lambda i: (0, 0))
    out = jax.ShapeDtypeStruct((n, D_MODEL), F32)
    return pl.pallas_call(
        _merge_kernel,
        grid=(n // tm,),
        in_specs=[rows(D_MODEL), rows(D_INNER), rows(N_HEADS * HEAD_DIM), rows(2 * D_MODEL, C_MERGE // (2 * D_MODEL)),
                  const((D_INNER, D_MODEL)), const((N_HEADS * HEAD_DIM, D_MODEL)), const((D_MODEL, D_MODEL)),
                  const((1, D_MODEL)), const((D_MODEL, PEER_HEADS * PEER_DK))],
        out_specs=[rows(D_MODEL), rows(D_MODEL), rows(PEER_HEADS * PEER_DK)],
        out_shape=[out, out, jax.ShapeDtypeStruct((n, PEER_HEADS * PEER_DK), F32)],
        compiler_params=_cparams(("arbitrary",)),
        name="merge",
    )(x2d, y_ssm, o_attn, u2d, wps, wpa, wo, norm2_w.reshape(1, D_MODEL), wpq)


def _top_k_rows(s, k, payload=None):
    r = s.shape[0]
    ridx = lax.broadcasted_iota(jnp.int32, s.shape, 0)
    vals, picks = [], []
    for _ in range(k):
        m = jnp.max(s, axis=0, keepdims=True)
        idx = jnp.min(jnp.where(s == m, ridx, r), axis=0, keepdims=True)
        chosen = ridx == idx
        vals.append(m)
        picks.append(idx if payload is None else jnp.sum(jnp.where(chosen, payload, 0), axis=0, keepdims=True))
        s = jnp.where(chosen, -jnp.inf, s)
    return jnp.concatenate(vals, axis=0), jnp.concatenate(picks, axis=0)


def _peer_select_kernel(pq_ref, sk_ref, e_ref, g_ref):
    half = PEER_DK // 2
    es, gs = [], []
    for hd in range(PEER_HEADS):
        tops = []
        for cidx in range(2):
            col = (hd * 2 + cidx) * half
            q = pq_ref[:, col:col + half].astype(BF16)
            tops.append(_top_k_rows(_nt(sk_ref[hd, cidx], q), PEER_TOPK))
        (s1, i1), (s2, i2) = tops
        n_b = [PEER_TOPK // (a + 1) for a in range(PEER_TOPK)]
        pad = -sum(n_b) % 8
        tn = s1.shape[1]
        cand = jnp.concatenate([s1[a:a + 1] + s2[:n_b[a]] for a in range(PEER_TOPK)]
                               + [jnp.full((pad, tn), -jnp.inf, F32)], axis=0)
        expert = jnp.concatenate([i1[a:a + 1] * N_KEYS + i2[:n_b[a]] for a in range(PEER_TOPK)]
                                 + [jnp.zeros((pad, tn), jnp.int32)], axis=0)
        top_s, top_e = _top_k_rows(cand, PEER_TOPK, payload=expert)
        es.append(top_e)
        ex = jnp.exp(top_s - top_s[0:1])
        gs.append(ex / jnp.sum(ex, axis=0, keepdims=True))
    e_ref[...] = jnp.concatenate(es, axis=0).T
    g_ref[...] = jnp.concatenate(gs, axis=0).T


def _peer_select(pq, sub_keys_bf):
    n = pq.shape[0]
    tn = min(256, n)
    nk = PEER_HEADS * PEER_TOPK
    return pl.pallas_call(
        _peer_select_kernel,
        grid=(n // tn,),
        in_specs=[pl.BlockSpec((tn, PEER_HEADS * PEER_DK), lambda i: (i, 0)),
                  pl.BlockSpec((PEER_HEADS, 2, N_KEYS, PEER_DK // 2), lambda i: (0, 0, 0, 0))],
        out_specs=[pl.BlockSpec((tn, nk), lambda i: (i, 0)), pl.BlockSpec((tn, nk), lambda i: (i, 0))],
        out_shape=[jax.ShapeDtypeStruct((n, nk), jnp.int32), jax.ShapeDtypeStruct((n, nk), F32)],
        compiler_params=_cparams(("arbitrary",)),
        name="peer_select",
    )(pq, sub_keys_bf)


PEER_TOK = 8


FEAT_TILES = D_MODEL // LANES
EXPERT_ROWS = 2 * FEAT_TILES


def _peer_gather_kernel(e_ref, en_ref, g_ref, h_ref, x_ref, uv_hbm, o_ref, buf_ref, sem_ref, *, n):
    i = pl.program_id(0)
    nk = PEER_HEADS * PEER_TOPK
    rows = PEER_TOK * nk * EXPERT_ROWS

    def issue_token(idx_ref, slot, t):
        for k in range(nk):
            src = pl.multiple_of(idx_ref[t, k] * EXPERT_ROWS, EXPERT_ROWS)
            dst = pl.multiple_of((t * nk + k) * EXPERT_ROWS, EXPERT_ROWS)
            pltpu.make_async_copy(uv_hbm.at[pl.ds(src, EXPERT_ROWS)],
                                  buf_ref.at[slot, pl.ds(dst, EXPERT_ROWS)], sem_ref.at[slot]).start()

    @pl.when(i == 0)
    def _():
        def tok_body(t, carry):
            issue_token(e_ref, 0, t)
            return carry
        lax.fori_loop(0, PEER_TOK, tok_body, 0)

    @pl.when(i + 1 < n)
    def _():
        for t in range(PEER_TOK):
            issue_token(en_ref, (i + 1) % 2, t)

    slot = i % 2
    pltpu.make_async_copy(uv_hbm.at[pl.ds(0, rows)], buf_ref.at[slot], sem_ref.at[slot]).wait()

    eye = (lax.broadcasted_iota(jnp.int32, (nk, nk), 0) == lax.broadcasted_iota(jnp.int32, (nk, nk), 1)).astype(F32)
    g_t = _nt(eye, g_ref[...], precision=HI)
    for t in range(PEER_TOK):
        tile_row = lambda s, t=t: buf_ref[slot, pl.ds(t * nk * EXPERT_ROWS + s, nk, stride=EXPERT_ROWS), :]
        prod = tile_row(0) * h_ref[t, 0:1, :]
        for s in range(1, FEAT_TILES):
            prod = prod + tile_row(s) * h_ref[t, s:s + 1, :]
        pre = jnp.sum(prod, axis=-1, keepdims=True)
        act = 0.5 * pre * (1.0 + lax.erf(pre * np.float32(math.sqrt(0.5))))
        coef = g_t[:, t:t + 1] * act
        out = jnp.concatenate([jnp.sum(coef * tile_row(FEAT_TILES + s), axis=0, keepdims=True)
                               for s in range(FEAT_TILES)], axis=0)
        o_ref[t] = x_ref[t] + out


def _peer_gather(e, g, h2, x2, uv):
    n = h2.shape[0]
    nk = PEER_HEADS * PEER_TOPK
    steps = n // PEER_TOK
    tiles = lambda: pl.BlockSpec((PEER_TOK, FEAT_TILES, LANES), lambda i: (i, 0, 0))
    smem = lambda imap: pl.BlockSpec((PEER_TOK, nk), imap, memory_space=pltpu.SMEM)
    return pl.pallas_call(
        functools.partial(_peer_gather_kernel, n=steps),
        grid=(steps,),
        in_specs=[smem(lambda i: (i, 0)), smem(lambda i: (jnp.minimum(i + 1, steps - 1), 0)),
                  pl.BlockSpec((PEER_TOK, nk), lambda i: (i, 0)), tiles(), tiles(),
                  pl.BlockSpec(memory_space=pl.ANY)],
        out_specs=tiles(),
        out_shape=jax.ShapeDtypeStruct((n, FEAT_TILES, LANES), F32),
        scratch_shapes=[pltpu.VMEM((2, PEER_TOK * nk * EXPERT_ROWS, LANES), F32), pltpu.SemaphoreType.DMA((2,))],
        compiler_params=_cparams(("arbitrary",)),
        name="peer_gather",
    )(e, e, g, h2.reshape(n, FEAT_TILES, LANES), x2.reshape(n, FEAT_TILES, LANES), uv).reshape(n, D_MODEL)


def _permute_w_in(w_in):
    sizes = (D_INNER, CONV_DIM, SSM_HEADS, N_HEADS * HEAD_DIM, 3 * ROW_W, 3 * N_HEADS, 2 * D_MODEL)
    cuts = np.cumsum((0,) + sizes)
    z, xbc, dt, q, kv, ng, mg = (w_in[:, cuts[i]:cuts[i + 1]] for i in range(7))
    pad = jnp.zeros((D_MODEL, LANES - SSM_HEADS - 3 * N_HEADS), w_in.dtype)
    return jnp.concatenate([xbc, z, mg, q, kv, dt, ng, pad], axis=1).astype(BF16)


def _pack_experts(expert_u, expert_v):
    n = expert_u.shape[0]
    tile = lambda a: a.reshape(n, FEAT_TILES, LANES)
    return jnp.concatenate([tile(expert_u), tile(expert_v)], axis=1).reshape(n * EXPERT_ROWS, LANES)


def _layer_weights(norm1_w, w_in, conv_w, conv_b, dt_bias, a_log, d_skip, ssm_norm_w, w_proj_ssm, q_norm_w,
                   k_norm_w, cmp_pos, w_cmp, w_proj_attn, w_out, norm2_w, w_pq, sub_keys, expert_u, expert_v):
    wbd, pos, kg = _compress_weights(w_cmp, cmp_pos, k_norm_w[0])
    return dict(norm1_w=norm1_w, w_in=_permute_w_in(w_in), conv_w=conv_w, conv_b=conv_b, dt_bias=dt_bias,
                a_log=a_log, d_skip=d_skip, ssm_norm_w=ssm_norm_w, wps=w_proj_ssm.astype(BF16),
                q_norm_w=q_norm_w, k_norm_w=k_norm_w, wbd=wbd, pos=pos, kg=kg, wpa=w_proj_attn.astype(BF16),
                wo=w_out.astype(BF16), norm2_w=norm2_w, wpq=w_pq.astype(BF16), sub_keys=sub_keys.astype(BF16),
                uv=_pack_experts(expert_u, expert_v))


def _token_tail(x2d, y_ssm, o_attn, u2d, w):
    x2, h2, pq = _merge(x2d, y_ssm, o_attn, u2d, w["wps"], w["wpa"], w["wo"], w["norm2_w"], w["wpq"])
    e, g = _peer_select(pq, w["sub_keys"])
    return _peer_gather(e, g, h2, x2, w["uv"])


def _layer_prompt(x, w, win_buf):
    b, t = x.shape[:2]
    assert t % (2 * Q_BLOCK) == 0 and t >= WINDOW + Q_BLOCK and t >= win_buf
    x2d = x.reshape(b * t, D_MODEL)
    u2d = _inproj(x2d, w["norm1_w"], w["w_in"])
    u3 = u2d.reshape(b, t, D_IN_P)
    y_ssm, h_last = _ssd(u3, jnp.zeros((b, D_INNER, D_STATE), F32), jnp.zeros((b, 8, CONV_DIM), F32),
                         w["conv_w"], w["conv_b"], w["dt_bias"], w["a_log"], w["d_skip"], w["ssm_norm_w"])
    cos_t, sin_t = _rope_tables(jnp.arange(t, dtype=jnp.int32))
    qn, qr, slc, win = _nsa_prep(u2d, cos_t, sin_t, w["q_norm_w"], w["k_norm_w"], t)
    cmp_rows = u3[:, :, C_KV:C_KV + ROW_W]
    kc, vc = _compress_prompt(cmp_rows, w["wbd"], w["pos"], w["kg"])
    hw = N_HEADS * HEAD_DIM
    o_attn = _nsa_prompt(qn.reshape(b, t, hw), qr.reshape(b, t, hw), u3, kc, vc,
                         slc.reshape(b, t, ROW_W), win.reshape(b, t, ROW_W))
    y = _token_tail(x2d, y_ssm.reshape(b * t, D_INNER), o_attn.reshape(b * t, hw), u2d, w)
    rows = lambda a: a.reshape(b, -1, 2, N_KV, HEAD_DIM)
    state = (rows(cmp_rows), rows(slc), rows(win.reshape(b, t, ROW_W)[:, t - win_buf:]),
             h_last.reshape(b, SSM_HEADS, SSM_HEADDIM, D_STATE), u3[:, t - (CONV_W - 1):, C_XBC:C_XBC + CONV_DIM])
    return y.reshape(b, t, D_MODEL), state


def _layer_sample(x, w, cache_cmp, cache_slc, page_table, win_state, h0, conv_state):
    b, t = x.shape[:2]
    assert t == 1
    past = page_table.shape[1] * PAGE_SIZE
    win_buf = win_state.shape[1]
    assert page_table.shape[1] % N_PARTS == 0 and win_buf >= 1
    x2d = x.reshape(b, D_MODEL)
    u2d = _inproj(x2d, w["norm1_w"], w["w_in"])
    u3 = u2d.reshape(b, 1, D_IN_P)
    prefix8 = jnp.pad(conv_state, ((0, 0), (8 - (CONV_W - 1), 0), (0, 0)))
    y_ssm, h_last = _ssd(u3, h0.reshape(b, D_INNER, D_STATE), prefix8, w["conv_w"], w["conv_b"], w["dt_bias"],
                         w["a_log"], w["d_skip"], w["ssm_norm_w"])
    cos_t, sin_t = _rope_tables(jnp.full((1,), past, jnp.int32))
    qn, qr, slc, win = _nsa_prep(u2d, cos_t, sin_t, w["q_norm_w"], w["k_norm_w"], 1)
    n_pool = cache_cmp.shape[0]
    page_t = lambda cache: jnp.transpose(cache, (0, 2, 3, 4, 1)).reshape(n_pool, ROW_W, PAGE_SIZE)
    o_attn = _nsa_sample_t(qn, qr, u2d, slc, win, win_state.reshape(b, win_buf, ROW_W), page_table,
                           page_t(cache_cmp), page_t(cache_slc), w["wbd"], w["pos"], w["kg"])
    y = _token_tail(x2d, y_ssm.reshape(b, D_INNER), o_attn, u2d, w)
    rows = lambda a: a.reshape(b, -1, 2, N_KV, HEAD_DIM)
    new_win = jnp.concatenate([win_state.reshape(b, win_buf, ROW_W), win.reshape(b, 1, ROW_W)], axis=1)[:, 1:]
    new_conv = jnp.concatenate([conv_state, u3[:, :, C_XBC:C_XBC + CONV_DIM]], axis=1)[:, 1:]
    state = (rows(u2d[:, C_KV:C_KV + ROW_W]), rows(slc), rows(new_win),
             h_last.reshape(b, SSM_HEADS, SSM_HEADDIM, D_STATE), new_conv)
    return y.reshape(b, 1, D_MODEL), state


def kernel(x_prompt, x_sample, cache_cmp_kv, cache_slc_kv, page_table, state_win_kv, state_ssm, state_conv,
           norm1_w, w_in, conv_w, conv_b, dt_bias, a_log, d_skip, ssm_norm_w, w_proj_ssm, q_norm_w, k_norm_w,
           cmp_pos, w_cmp, w_proj_attn, w_out, norm2_w, w_pq, sub_keys, expert_u, expert_v):
    depth = w_in.shape[0]
    win_buf = state_win_kv.shape[2]
    yp, ys = x_prompt, x_sample
    new_p, new_s = [], []
    for l in range(depth):
        w = _layer_weights(norm1_w[l], w_in[l], conv_w[l], conv_b[l], dt_bias[l], a_log[l], d_skip[l],
                           ssm_norm_w[l], w_proj_ssm[l], q_norm_w[l], k_norm_w[l], cmp_pos[l], w_cmp[l],
                           w_proj_attn[l], w_out[l], norm2_w[l], w_pq[l], sub_keys[l], expert_u[l], expert_v[l])
        yp, st_p = _layer_prompt(yp, w, win_buf)
        ys, st_s = _layer_sample(ys, w, cache_cmp_kv[l], cache_slc_kv[l], page_table, state_win_kv[l],
                                 state_ssm[l], state_conv[l])
        new_p.append(st_p)
        new_s.append(st_s)
    stack = lambda states, i: jnp.stack([s[i] for s in states], axis=0)
    return (yp, ys, stack(new_p, 0), stack(new_s, 0), stack(new_p, 1), stack(new_s, 1), stack(new_p, 2),
            stack(new_s, 2), stack(new_p, 3), stack(new_s, 3), stack(new_p, 4), stack(new_s, 4))
```

```python
import functools
import math

import jax
import jax.numpy as jnp
import numpy as np
from jax import lax
from jax.experimental import pallas as pl
from jax.experimental.pallas import tpu as pltpu

F32 = jnp.float32
BF16 = jnp.bfloat16
HI = lax.Precision.HIGHEST

D_MODEL = 1024
PAGE_SIZE = 128
D_INNER = 2048
SSM_HEADDIM = 64
SSM_HEADS = 32
SSM_GROUPS = 8
SSM_HPG = 4
D_STATE = 128
CONV_W = 4
CONV_DIM = 4096
SSD_CHUNK = 128
N_HEADS = 16
N_KV = 4
GQA = 4
HEAD_DIM = 64
CMP_STRIDE = 16
CMP_LEN = 32
SLC_BLOCK = 64
SUB_PER_SLC = 4
N_SEL = 16
WINDOW = 512
Q_BLOCK = 128
ROPE_THETA = 10000.0
ATTN_SCALE = HEAD_DIM ** -0.5
PEER_HEADS = 8
N_KEYS = 128
PEER_TOPK = 16
PEER_DK = 128
EPS = 1e-6
NEG = -1e9
FORCE_BONUS = 1e3

LANES = 128
KV_W = N_KV * HEAD_DIM
ROW_W = 2 * KV_W
ROW_TILES = ROW_W // LANES
C_XBC, C_Z, C_MERGE, C_Q, C_KV, C_SMALL = 0, 4096, 6144, 8192, 9216, 10752
D_IN_P = 10880
IN_TILE_N = 2176
VMEM_LIMIT = 56 * 1024 * 1024


def _cparams(sem, vmem=VMEM_LIMIT):
    return pltpu.CompilerParams(dimension_semantics=sem, vmem_limit_bytes=vmem)


def _nt(a, b, precision=None):
    return lax.dot_general(a, b, (((1,), (1,)), ((), ())), preferred_element_type=F32, precision=precision)


def _tn(a, b, precision=None):
    return lax.dot_general(a, b, (((0,), (0,)), ((), ())), preferred_element_type=F32, precision=precision)


def _mm(a, b, precision=None):
    return jnp.dot(a, b, preferred_element_type=F32, precision=precision)


def _sigmoid(x):
    return 1.0 / (1.0 + jnp.exp(-x))


def _silu(x):
    return x * _sigmoid(x)


def _softplus(x):
    return jnp.maximum(x, 0.0) + jnp.log1p(jnp.exp(-jnp.abs(x)))


def _inproj_kernel(x_ref, nw_ref, w_ref, o_ref):
    x = x_ref[...]
    h = x * lax.rsqrt(jnp.mean(x * x, axis=-1, keepdims=True) + EPS) * nw_ref[...]
    o_ref[...] = _mm(h.astype(BF16), w_ref[...])


def _inproj(x2d, norm_w, w_p):
    n = x2d.shape[0]
    tm = min(512, n)
    return pl.pallas_call(
        _inproj_kernel,
        grid=(D_IN_P // IN_TILE_N, n // tm),
        in_specs=[pl.BlockSpec((tm, D_MODEL), lambda j, i: (i, 0)),
                  pl.BlockSpec((1, D_MODEL), lambda j, i: (0, 0)),
                  pl.BlockSpec((D_MODEL, IN_TILE_N), lambda j, i: (0, j))],
        out_specs=pl.BlockSpec((tm, IN_TILE_N), lambda j, i: (i, j)),
        out_shape=jax.ShapeDtypeStruct((n, D_IN_P), F32),
        compiler_params=_cparams(("arbitrary", "arbitrary")),
        name="inproj",
    )(x2d, norm_w.reshape(1, D_MODEL), w_p)


def _ssd_kernel(xbc_ref, z_ref, sm_ref, h0_ref, pre_ref, cw_ref, cb_ref, dtb_ref, alog_ref, dsk_ref, nw_ref,
                y_ref, hl_ref, st_ref, xe_ref, *, q, qp):
    c = pl.program_id(1)

    @pl.when(c == 0)
    def _():
        st_ref[...] = h0_ref[0]
        xe_ref[0:8, :] = pre_ref[0]

    @pl.when(c > 0)
    def _():
        xe_ref[0:8, :] = xe_ref[qp:qp + 8, :]

    xe_ref[8:8 + q, :] = xbc_ref[0]
    if qp > q:
        xe_ref[8 + q:8 + qp, :] = jnp.zeros((qp - q, CONV_DIM), F32)

    conv = cb_ref[...] + xe_ref[8:8 + qp, :] * cw_ref[3:4, :]
    for s in range(1, CONV_W):
        conv = conv + xe_ref[8 - s:8 - s + qp, :] * cw_ref[3 - s:4 - s, :]
    act = _silu(conv)

    lane = lax.broadcasted_iota(jnp.int32, (1, LANES), 1)
    head_lane = lane < SSM_HEADS
    sm = sm_ref[0]
    if qp > q:
        sm = jnp.concatenate([sm, jnp.zeros((qp - q, LANES), F32)], axis=0)
    dt = _softplus(sm + dtb_ref[...])
    dt = jnp.where(head_lane, dt, 0.0)
    if qp > q:
        row = lax.broadcasted_iota(jnp.int32, (qp, 1), 0)
        dt = jnp.where(row < q, dt, 0.0)
    a = jnp.where(head_lane, -jnp.exp(alog_ref[...]), 0.0)
    da = dt * a
    ri = lax.broadcasted_iota(jnp.int32, (qp, qp), 0)
    ci = lax.broadcasted_iota(jnp.int32, (qp, qp), 1)
    causal = ci <= ri
    cs = _mm(causal.astype(F32), da, precision=HI)
    eye = (lax.broadcasted_iota(jnp.int32, (LANES, LANES), 0)
           == lax.broadcasted_iota(jnp.int32, (LANES, LANES), 1)).astype(F32)
    cs_t = _nt(eye, cs, precision=HI)
    cs_last = cs[qp - 1:qp, :]
    e_cs = jnp.exp(cs)
    e_end = jnp.exp(cs_last - cs)
    e_last = jnp.exp(cs_last)

    for g in range(SSM_GROUPS):
        bm = act[:, D_INNER + g * D_STATE:D_INNER + (g + 1) * D_STATE].astype(BF16)
        cm = act[:, D_INNER + SSM_GROUPS * D_STATE + g * D_STATE:
                 D_INNER + SSM_GROUPS * D_STATE + (g + 1) * D_STATE].astype(BF16)
        cbm = _nt(cm, bm)
        r0 = g * SSM_HPG * SSM_HEADDIM
        s_g = st_ref[r0:r0 + SSM_HPG * SSM_HEADDIM, :]
        y_off = _nt(cm, s_g.astype(BF16))
        ys, xds, decs = [], [], []
        for r in range(SSM_HPG):
            h = g * SSM_HPG + r
            xs = act[:, h * SSM_HEADDIM:(h + 1) * SSM_HEADDIM]
            col = cs[:, h:h + 1]
            rowv = cs_t[h:h + 1, :]
            lmat = jnp.exp(jnp.where(causal, col - rowv, -jnp.inf))
            xdt = xs * dt[:, h:h + 1]
            y_d = _mm((cbm * lmat).astype(BF16), xdt.astype(BF16))
            ys.append(y_d + y_off[:, r * SSM_HEADDIM:(r + 1) * SSM_HEADDIM] * e_cs[:, h:h + 1]
                      + xs * dsk_ref[:, h:h + 1])
            xds.append(xdt * e_end[:, h:h + 1])
            decs.append(jnp.broadcast_to(e_last[:, h:h + 1], (SSM_HEADDIM, 1)))
        xd = jnp.concatenate(xds, axis=1)
        new = _tn(xd.astype(BF16), bm)
        dec = jnp.concatenate(decs, axis=0)
        st_ref[r0:r0 + SSM_HPG * SSM_HEADDIM, :] = dec * s_g + new
        yg = jnp.concatenate(ys, axis=1)
        w = D_INNER // SSM_GROUPS
        zg = z_ref[0][:, g * w:(g + 1) * w]
        if qp > q:
            yg = yg[:q]
        yg = yg * _silu(zg)
        yg = yg * lax.rsqrt(jnp.mean(yg * yg, axis=-1, keepdims=True) + EPS) * nw_ref[:, g * w:(g + 1) * w]
        y_ref[0, :, g * w:(g + 1) * w] = yg

    @pl.when(c == pl.num_programs(1) - 1)
    def _():
        hl_ref[0] = st_ref[...]


def _pad_lanes(v, width=LANES):
    v = v.reshape(1, -1)
    return jnp.pad(v, ((0, 0), (0, width - v.shape[1])))


def _ssd(u3, h0, prefix8, conv_w, conv_b, dt_bias, a_log, d_skip, ssm_norm_w):
    b, t = u3.shape[:2]
    q = min(SSD_CHUNK, t)
    qp = max(q, 8)
    nc = t // q
    row = lambda shape: pl.BlockSpec(shape, lambda i, c: (0, 0))
    return pl.pallas_call(
        functools.partial(_ssd_kernel, q=q, qp=qp),
        grid=(b, nc),
        in_specs=[pl.BlockSpec((1, q, CONV_DIM), lambda i, c: (i, c, C_XBC // CONV_DIM)),
                  pl.BlockSpec((1, q, D_INNER), lambda i, c: (i, c, C_Z // D_INNER)),
                  pl.BlockSpec((1, q, LANES), lambda i, c: (i, c, C_SMALL // LANES)),
                  pl.BlockSpec((1, D_INNER, D_STATE), lambda i, c: (i, 0, 0)),
                  pl.BlockSpec((1, 8, CONV_DIM), lambda i, c: (i, 0, 0)),
                  row((CONV_W, CONV_DIM)), row((1, CONV_DIM)), row((1, LANES)), row((1, LANES)), row((1, LANES)),
                  row((1, D_INNER))],
        out_specs=[pl.BlockSpec((1, q, D_INNER), lambda i, c: (i, c, 0)),
                   pl.BlockSpec((1, D_INNER, D_STATE), lambda i, c: (i, 0, 0))],
        out_shape=[jax.ShapeDtypeStruct((b, t, D_INNER), F32),
                   jax.ShapeDtypeStruct((b, D_INNER, D_STATE), F32)],
        scratch_shapes=[pltpu.VMEM((D_INNER, D_STATE), F32), pltpu.VMEM((qp + 8, CONV_DIM), F32)],
        compiler_params=_cparams(("arbitrary", "arbitrary")),
        name="ssd",
    )(u3, u3, u3, h0, prefix8, conv_w, conv_b.reshape(1, CONV_DIM), _pad_lanes(dt_bias), _pad_lanes(a_log),
      _pad_lanes(d_skip), ssm_norm_w.reshape(1, D_INNER))


def _head_sumsq(x, bd):
    sq = x * x
    hi = sq.astype(BF16)
    lo = (sq - hi.astype(F32)).astype(BF16)
    return _mm(hi, bd) + _mm(lo, bd)


def _rope_tiles(x, cos, sin_signed):
    n_tiles = x.shape[1] // LANES
    lane = lax.broadcasted_iota(jnp.int32, (1, LANES), 1)
    first_half = (lane % HEAD_DIM) < (HEAD_DIM // 2)
    outs = []
    for i in range(n_tiles):
        xt = x[:, i * LANES:(i + 1) * LANES]
        swapped = jnp.where(first_half, pltpu.roll(xt, LANES - HEAD_DIM // 2, 1), pltpu.roll(xt, HEAD_DIM // 2, 1))
        outs.append(xt * cos + swapped * sin_signed)
    return jnp.concatenate(outs, axis=1)


def _nsa_prep_kernel(q_ref, kv_ref, cos_ref, sin_ref, bd_ref, qw_ref, kw_ref, qn_ref, qr_ref, slc_ref, win_ref):
    cos = cos_ref[...]
    sin = sin_ref[...]
    bd = bd_ref[...]
    q = q_ref[...]
    qn = q * lax.rsqrt(_head_sumsq(q, bd) * (1.0 / HEAD_DIM) + EPS) * qw_ref[...]
    qn_ref[...] = qn
    qr_ref[...] = _rope_tiles(qn, cos, sin)
    for br, o_ref in ((1, slc_ref), (2, win_ref)):
        k = kv_ref[:, br * ROW_W:br * ROW_W + KV_W]
        kn = k * lax.rsqrt(_head_sumsq(k, bd[:KV_W, :KV_W]) * (1.0 / HEAD_DIM) + EPS) * kw_ref[br:br + 1, :]
        o_ref[:, :KV_W] = _rope_tiles(kn, cos, sin)
        o_ref[:, KV_W:] = kv_ref[:, br * ROW_W + KV_W:(br + 1) * ROW_W]


def _nsa_prep(u2d, cos_t, sin_t, q_norm_w, k_norm_w, t):
    n = u2d.shape[0]
    tm = min(512, n, t) if t > 1 else n
    tab_blocks = max(t // tm, 1)
    if t == 1:
        cos_t = jnp.broadcast_to(cos_t, (tm, LANES))
        sin_t = jnp.broadcast_to(sin_t, (tm, LANES))
    head_id = np.arange(N_HEADS * HEAD_DIM) // HEAD_DIM
    bd = jnp.asarray(head_id[:, None] == head_id[None, :], BF16)
    qw = jnp.tile(q_norm_w, N_HEADS).reshape(1, -1)
    kw = jnp.tile(k_norm_w, (1, N_KV))
    tab = lambda: pl.BlockSpec((tm, LANES), lambda i: (i % tab_blocks, 0))
    return pl.pallas_call(
        _nsa_prep_kernel,
        grid=(n // tm,),
        in_specs=[pl.BlockSpec((tm, N_HEADS * HEAD_DIM), lambda i: (i, C_Q // (N_HEADS * HEAD_DIM))),
                  pl.BlockSpec((tm, 3 * ROW_W), lambda i: (i, C_KV // (3 * ROW_W))),
                  tab(), tab(),
                  pl.BlockSpec((N_HEADS * HEAD_DIM, N_HEADS * HEAD_DIM), lambda i: (0, 0)),
                  pl.BlockSpec((1, N_HEADS * HEAD_DIM), lambda i: (0, 0)),
                  pl.BlockSpec((3, KV_W), lambda i: (0, 0))],
        out_specs=[pl.BlockSpec((tm, N_HEADS * HEAD_DIM), lambda i: (i, 0)),
                   pl.BlockSpec((tm, N_HEADS * HEAD_DIM), lambda i: (i, 0)),
                   pl.BlockSpec((tm, ROW_W), lambda i: (i, 0)),
                   pl.BlockSpec((tm, ROW_W), lambda i: (i, 0))],
        out_shape=[jax.ShapeDtypeStruct((n, N_HEADS * HEAD_DIM), F32),
                   jax.ShapeDtypeStruct((n, N_HEADS * HEAD_DIM), F32),
                   jax.ShapeDtypeStruct((n, ROW_W), F32),
                   jax.ShapeDtypeStruct((n, ROW_W), F32)],
        compiler_params=_cparams(("arbitrary",)),
        name="nsa_prep",
    )(u2d, u2d, cos_t, sin_t, bd, qw, kw)


def _rope_tables(pos):
    half = HEAD_DIM // 2
    inv = ROPE_THETA ** (-jnp.arange(half, dtype=F32) / half)
    ang = pos.astype(F32)[:, None] * inv[None, :]
    cos = jnp.cos(ang)
    sin = jnp.sin(ang)
    reps = LANES // HEAD_DIM
    return (jnp.tile(jnp.concatenate([cos, cos], axis=1), (1, reps)),
            jnp.tile(jnp.concatenate([-sin, sin], axis=1), (1, reps)))


def _sub_block_proj(load_rows, wbd_ref, pos_ref):
    acc = [[None, None], [None, None]]
    per = KV_W // LANES
    for l in range(CMP_STRIDE):
        for cidx in range(2):
            rows = jnp.concatenate([load_rows(l, cidx * per + j) for j in range(per)], axis=1)
            for half in range(2):
                x = rows + pos_ref[half * CMP_STRIDE + l, cidx:cidx + 1, :]
                part = _mm(x.astype(BF16), wbd_ref[half * CMP_STRIDE + l, cidx])
                acc[half][cidx] = part if acc[half][cidx] is None else acc[half][cidx] + part
    return acc


def _compress_finish(first_k, first_v, second_k, second_v, kg):
    n_sub = first_k.shape[0]
    k = first_k + pltpu.roll(second_k, n_sub - 1, 0)
    v = first_v + pltpu.roll(second_v, n_sub - 1, 0)
    parts = []
    for hd in range(N_KV):
        kh = k[:, hd * HEAD_DIM:(hd + 1) * HEAD_DIM]
        parts.append(kh * lax.rsqrt(jnp.mean(kh * kh, axis=-1, keepdims=True) + EPS))
    return jnp.concatenate(parts, axis=1) * kg, v


def _compress_kernel(rows_ref, wbd_ref, pos_ref, kg_ref, kc_ref, vc_ref, *, n_sub):
    acc = _sub_block_proj(lambda l, j: rows_ref[0, pl.ds(l * ROW_TILES + j, n_sub, stride=CMP_STRIDE * ROW_TILES), :],
                          wbd_ref, pos_ref)
    kc, vc = _compress_finish(acc[0][0], acc[0][1], acc[1][0], acc[1][1], kg_ref[...])
    kc_ref[0] = kc
    vc_ref[0] = vc


def _compress_weights(w_cmp, cmp_pos, k_gain):
    eye = jnp.eye(N_KV, dtype=F32)
    wbd = jnp.einsum("hg,lcde->lchdge", eye, w_cmp).reshape(CMP_LEN, 2, KV_W, KV_W).astype(BF16)
    pos = jnp.tile(cmp_pos, (1, 1, N_KV))
    kg = jnp.tile(k_gain, N_KV).reshape(1, KV_W)
    return wbd, pos, kg


def _compress_prompt(cmp_rows, wbd, pos, kg):
    b, t = cmp_rows.shape[:2]
    cmp_rows = cmp_rows.reshape(b, t * ROW_TILES, LANES)
    n_sub = t // CMP_STRIDE
    const = lambda shape: pl.BlockSpec(shape, lambda i: (0,) * len(shape))
    return pl.pallas_call(
        functools.partial(_compress_kernel, n_sub=n_sub),
        grid=(b,),
        in_specs=[pl.BlockSpec((1, t * ROW_TILES, LANES), lambda i: (i, 0, 0)),
                  const((CMP_LEN, 2, KV_W, KV_W)), const((CMP_LEN, 2, KV_W)), const((1, KV_W))],
        out_specs=[pl.BlockSpec((1, n_sub, KV_W), lambda i: (i, 0, 0)),
                   pl.BlockSpec((1, n_sub, KV_W), lambda i: (i, 0, 0))],
        out_shape=[jax.ShapeDtypeStruct((b, n_sub, KV_W), F32), jax.ShapeDtypeStruct((b, n_sub, KV_W), F32)],
        compiler_params=_cparams(("arbitrary",)),
        name="compress_prompt",
    )(cmp_rows, wbd, pos, kg)


def _softmax_rows(s, mask):
    s = jnp.where(mask, s, NEG)
    m = jnp.max(s, axis=-1, keepdims=True)
    e = jnp.exp(s - m)
    p = e / jnp.sum(e, axis=-1, keepdims=True)
    return jnp.where(mask, p, 0.0)


def _stack_heads(x, kv):
    return jnp.concatenate([x[:, (kv * GQA + g) * HEAD_DIM:(kv * GQA + g + 1) * HEAD_DIM] for g in range(GQA)],
                           axis=0)


def _block_scores_t(p, tq, n_cmp, n_blk, sel_mat_t):
    imp = p[0:tq]
    for g in range(1, GQA):
        imp = imp + p[g * tq:(g + 1) * tq]
    return _nt(sel_mat_t, imp, precision=HI)


def _top_rank_mask_t(score_t, n_blk):
    jidx = lax.broadcasted_iota(jnp.int32, score_t.shape, 0)
    cnt = jnp.zeros(score_t.shape, F32)
    for j in range(n_blk):
        rowv = score_t[j:j + 1, :]
        beats = (rowv > score_t) | ((rowv == score_t) & (j < jidx))
        cnt = cnt + beats.astype(F32)
    return cnt < float(N_SEL)


def _nsa_prompt_kernel(qn_ref, qr_ref, sm_ref, kc_ref, vc_ref, slc_ref, win_ref, selm_ref, exp_ref, o_ref,
                       selk_ref, *, t, n_cmp, n_blk):
    tq = Q_BLOCK
    n = pl.program_id(1)
    start = n * tq
    qn = qn_ref[0]
    qr = qr_ref[0]
    gates = _sigmoid(sm_ref[0])
    q_pos1 = start + lax.broadcasted_iota(jnp.int32, (tq, 1), 0)
    q_pos = jnp.concatenate([q_pos1] * GQA, axis=0)
    ci = lax.broadcasted_iota(jnp.int32, (1, n_cmp), 1)
    c_mask = (ci * CMP_STRIDE + (CMP_LEN - 1)) <= q_pos
    jcol = lax.broadcasted_iota(jnp.int32, (n_blk, 1), 0)
    qrow = start + lax.broadcasted_iota(jnp.int32, (1, tq), 1)
    cur = qrow // SLC_BLOCK
    forced_t = (jcol == 0) | (jcol == cur) | (jcol == cur - 1)
    valid_t = jcol * SLC_BLOCK <= qrow
    w_lo = jnp.maximum(start - WINDOW, 0)
    w_len = WINDOW + tq
    kp_w = w_lo + lax.broadcasted_iota(jnp.int32, (1, w_len), 1)
    w_bias = jnp.where((kp_w <= q_pos1) & (kp_w > q_pos1 - WINDOW), 0.0, NEG)
    tk = 2 * tq
    n_tiles = (n + 2) // 2
    kp_all = lax.broadcasted_iota(jnp.int32, (1, t), 1)
    eye_q = (lax.broadcasted_iota(jnp.int32, (tq, tq), 0)
             == lax.broadcasted_iota(jnp.int32, (tq, tq), 1)).astype(BF16)

    outs = []
    for kv in range(N_KV):
        qn_s = _stack_heads(qn, kv).astype(BF16)
        qr_s = (_stack_heads(qr, kv) * ATTN_SCALE).astype(BF16)
        lanes = slice(kv * HEAD_DIM, (kv + 1) * HEAD_DIM)
        vlanes = slice(KV_W + kv * HEAD_DIM, KV_W + (kv + 1) * HEAD_DIM)
        kc = kc_ref[0][:, lanes].astype(BF16)
        vc = vc_ref[0][:, lanes].astype(BF16)
        p = _softmax_rows(_nt(qn_s, kc) * ATTN_SCALE, c_mask)
        o_cmp = _mm(p.astype(BF16), vc)
        score_t = _block_scores_t(p, tq, n_cmp, n_blk, selm_ref[...])
        score_t = jnp.where(forced_t, score_t + FORCE_BONUS, score_t)
        score_t = jnp.where(valid_t, score_t, NEG)
        sel_t = _top_rank_mask_t(score_t, n_blk) & valid_t
        sel = _nt(eye_q, jnp.where(sel_t, 1.0, 0.0).astype(BF16))
        sel_keys = _mm(sel.astype(BF16), exp_ref[...])
        selk_ref[...] = jnp.where((sel_keys > 0.5) & (kp_all <= q_pos1), 0.0, NEG)

        def tile_body(i, carry, lanes=lanes, vlanes=vlanes, qr_s=qr_s):
            m, l, acc = carry
            k0 = pl.multiple_of(i * tk, tk)
            kt = slc_ref[0, pl.ds(k0, tk), lanes].astype(BF16)
            vt = slc_ref[0, pl.ds(k0, tk), vlanes].astype(BF16)
            bias = selk_ref[:, pl.ds(k0, tk)]
            s = _nt(qr_s, kt) + jnp.concatenate([bias] * GQA, axis=0)
            m_new = jnp.maximum(m, jnp.max(s, axis=-1, keepdims=True))
            alpha = jnp.exp(m - m_new)
            e = jnp.exp(s - m_new)
            return (m_new, alpha * l + jnp.sum(e, axis=-1, keepdims=True),
                    alpha * acc + _mm(e.astype(BF16), vt))

        m0 = jnp.full((GQA * tq, 1), NEG, F32)
        l0 = jnp.zeros((GQA * tq, 1), F32)
        a0 = jnp.zeros((GQA * tq, HEAD_DIM), F32)
        _, l_s, acc_s = lax.fori_loop(0, n_tiles, tile_body, (m0, l0, a0))
        o_slc = acc_s / l_s
        kw = win_ref[0, pl.ds(pl.multiple_of(w_lo, tq), w_len), lanes].astype(BF16)
        vw = win_ref[0, pl.ds(pl.multiple_of(w_lo, tq), w_len), vlanes].astype(BF16)
        sw = _nt(qr_s, kw) + jnp.concatenate([w_bias] * GQA, axis=0)
        ew = jnp.exp(sw - jnp.max(sw, axis=-1, keepdims=True))
        pw = ew / jnp.sum(ew, axis=-1, keepdims=True)
        o_win = _mm(pw.astype(BF16), vw)
        for g in range(GQA):
            hd = kv * GQA + g
            rows = slice(g * tq, (g + 1) * tq)
            g0 = gates[:, SSM_HEADS + hd:SSM_HEADS + hd + 1]
            g1 = gates[:, SSM_HEADS + N_HEADS + hd:SSM_HEADS + N_HEADS + hd + 1]
            g2 = gates[:, SSM_HEADS + 2 * N_HEADS + hd:SSM_HEADS + 2 * N_HEADS + hd + 1]
            outs.append(g0 * o_cmp[rows] + g1 * o_slc[rows] + g2 * o_win[rows])
    o_ref[0] = jnp.concatenate(outs, axis=1)


def _selection_matrices(n_cmp, n_blk, n_keys):
    i = np.arange(n_cmp)[None, :]
    j = np.arange(n_blk)[:, None]
    sel = ((i >= SUB_PER_SLC * j - 1) & (i <= SUB_PER_SLC * j + SUB_PER_SLC - 1)).astype(np.float32)
    key = np.arange(n_keys)[None, :]
    expand = (key // SLC_BLOCK == j).astype(np.float32)
    return jnp.asarray(sel), jnp.asarray(expand, BF16)


def _nsa_prompt(qn3, qr3, u3, kc, vc, slc3, win3):
    b, t = qn3.shape[:2]
    n_cmp = t // CMP_STRIDE
    n_blk = t // SLC_BLOCK
    selm, expand = _selection_matrices(n_cmp, n_blk, t)
    selm = selm.at[:, n_cmp - 1].set(0.0)
    hw = N_HEADS * HEAD_DIM
    full = lambda w: pl.BlockSpec((1, t, w), lambda i, n: (i, 0, 0))
    return pl.pallas_call(
        functools.partial(_nsa_prompt_kernel, t=t, n_cmp=n_cmp, n_blk=n_blk),
        grid=(b, t // Q_BLOCK),
        in_specs=[pl.BlockSpec((1, Q_BLOCK, hw), lambda i, n: (i, n, 0)),
                  pl.BlockSpec((1, Q_BLOCK, hw), lambda i, n: (i, n, 0)),
                  pl.BlockSpec((1, Q_BLOCK, LANES), lambda i, n: (i, n, C_SMALL // LANES)),
                  pl.BlockSpec((1, n_cmp, KV_W), lambda i, n: (i, 0, 0)),
                  pl.BlockSpec((1, n_cmp, KV_W), lambda i, n: (i, 0, 0)),
                  full(ROW_W), full(ROW_W),
                  pl.BlockSpec((n_blk, n_cmp), lambda i, n: (0, 0)),
                  pl.BlockSpec((n_blk, t), lambda i, n: (0, 0))],
        out_specs=pl.BlockSpec((1, Q_BLOCK, hw), lambda i, n: (i, n, 0)),
        out_shape=jax.ShapeDtypeStruct((b, t, hw), F32),
        scratch_shapes=[pltpu.VMEM((Q_BLOCK, t), F32)],
        compiler_params=_cparams(("arbitrary", "arbitrary")),
        name="nsa_prompt",
    )(qn3, qr3, u3, kc, vc, slc3, win3, selm, expand)


N_PARTS = 2


def _head_block_diag(x_ref, scale):
    q16 = jnp.concatenate([x_ref[0, :, hd * HEAD_DIM:(hd + 1) * HEAD_DIM] for hd in range(N_HEADS)], axis=0) * scale
    kv_of_row = lax.broadcasted_iota(jnp.int32, (N_HEADS, HEAD_DIM), 0) // GQA
    return jnp.concatenate([jnp.where(kv_of_row == kv, q16, 0.0) for kv in range(N_KV)], axis=1)


def _nsa_sample_t_kernel(pt_ref, qn_ref, qr_ref, sm_ref, slcn_ref, winn_ref, wins_ref, wbd_ref, pos_ref, kg_ref,
                         selm_ref, cmp_hbm, slc_hbm, o_ref,
                         buf_ref, sem_ref, rows_ref, fs_ref, sel_ref, ocmp_ref, m_ref, l_ref, acc_ref,
                         *, past, ppp, n_blk_pad, batch):
    b = pl.program_id(0)
    s = pl.program_id(1)
    n_steps = 2 * N_PARTS
    c = b * n_steps + s
    total = batch * n_steps
    rp = ppp * PAGE_SIZE
    nsp = rp // CMP_STRIDE
    n_sub = N_PARTS * nsp
    win_buf = wins_ref.shape[1]
    cur = past // SLC_BLOCK

    def page_copy(hbm, bb, part, pg, slot):
        return pltpu.make_async_copy(hbm.at[pt_ref[bb, part * ppp + pg]],
                                     buf_ref.at[slot, pl.ds(pg * ROW_W, ROW_W)], sem_ref.at[slot])

    def start_chunk(cc, slot):
        bb = cc // n_steps
        ss = cc % n_steps

        @pl.when(ss < N_PARTS)
        def _():
            for pg in range(ppp):
                page_copy(cmp_hbm, bb, ss, pg, slot).start()

        @pl.when(ss >= N_PARTS)
        def _():
            for pg in range(ppp):
                page_copy(slc_hbm, bb, ss - N_PARTS, pg, slot).start()

    @pl.when(c == 0)
    def _():
        start_chunk(c, 0)

    @pl.when(c + 1 < total)
    def _():
        start_chunk(c + 1, (c + 1) % 2)

    slot = c % 2
    for pg in range(ppp):
        page_copy(cmp_hbm, b, 0, pg, slot).wait()

    @pl.when(s < N_PARTS)
    def _():
        for pg in range(ppp):
            for j in range(ROW_TILES):
                tile = buf_ref[slot, pg * ROW_W + j * LANES:pg * ROW_W + (j + 1) * LANES, :]
                rows_ref[j, pg * PAGE_SIZE:(pg + 1) * PAGE_SIZE, :] = tile.T
        acc = _sub_block_proj(lambda l, j: rows_ref[j, pl.ds(l, nsp, stride=CMP_STRIDE), :], wbd_ref, pos_ref)
        row0 = pl.multiple_of(s * nsp, nsp)
        fs_ref[0, pl.ds(row0, nsp), :] = acc[0][0]
        fs_ref[1, pl.ds(row0, nsp), :] = acc[0][1]
        fs_ref[2, pl.ds(row0, nsp), :] = acc[1][0]
        fs_ref[3, pl.ds(row0, nsp), :] = acc[1][1]

    @pl.when(s == N_PARTS - 1)
    def _():
        kc, vc = _compress_finish(fs_ref[0], fs_ref[1], fs_ref[2], fs_ref[3], kg_ref[...])
        ci = lax.broadcasted_iota(jnp.int32, (1, n_sub), 1)
        c_mask = (ci * CMP_STRIDE + (CMP_LEN - 1)) <= past
        qn_bd = _head_block_diag(qn_ref, 1.0).astype(BF16)
        p = _softmax_rows(_nt(qn_bd, kc.astype(BF16)) * ATTN_SCALE, c_mask)
        ocmp_ref[...] = _mm(p.astype(BF16), vc.astype(BF16))
        imps = [p[kv * GQA:kv * GQA + 1] + p[kv * GQA + 1:kv * GQA + 2] + p[kv * GQA + 2:kv * GQA + 3]
                + p[kv * GQA + 3:kv * GQA + 4] for kv in range(N_KV)]
        imp = jnp.concatenate(imps + [jnp.zeros((8 - N_KV, n_sub), F32)], axis=0)
        score = _mm(imp, selm_ref[...], precision=HI)
        jrow = lax.broadcasted_iota(jnp.int32, (1, n_blk_pad), 1)
        forced = (jrow == 0) | (jrow == cur) | (jrow == cur - 1)
        valid = jrow * SLC_BLOCK <= past
        score = jnp.where(forced, score + FORCE_BONUS, score)
        score = jnp.where(valid, score, NEG)
        eye = (lax.broadcasted_iota(jnp.int32, (n_blk_pad, n_blk_pad), 0)
               == lax.broadcasted_iota(jnp.int32, (n_blk_pad, n_blk_pad), 1)).astype(F32)
        score_c = _nt(eye, score, precision=HI)
        jc = lax.broadcasted_iota(jnp.int32, (n_blk_pad, n_blk_pad), 0)
        jr = lax.broadcasted_iota(jnp.int32, (n_blk_pad, n_blk_pad), 1)
        sels = []
        for kv in range(N_KV):
            colv = score_c[:, kv:kv + 1]
            rowv = score[kv:kv + 1, :]
            beats = (colv > rowv) | ((colv == rowv) & (jc < jr))
            rank = jnp.sum(beats.astype(F32), axis=0, keepdims=True)
            sel = jnp.where((rank < float(N_SEL)) & valid, 1.0, 0.0)
            sels.extend([sel] * GQA)
        sel_ref[...] = jnp.concatenate(sels, axis=0)

    @pl.when(s >= N_PARTS)
    def _():
        part = s - N_PARTS

        @pl.when(part == 0)
        def _():
            m_ref[...] = jnp.full(m_ref.shape, NEG, F32)
            l_ref[...] = jnp.zeros(l_ref.shape, F32)
            acc_ref[...] = jnp.zeros(acc_ref.shape, F32)

        key = part * rp + lax.broadcasted_iota(jnp.int32, (n_blk_pad, rp), 1)
        blk = lax.broadcasted_iota(jnp.int32, (n_blk_pad, rp), 0)
        expand = jnp.where(key // SLC_BLOCK == blk, 1.0, 0.0).astype(BF16)
        ok = _mm(sel_ref[...].astype(BF16), expand) > 0.5
        qr_bd = _head_block_diag(qr_ref, ATTN_SCALE).astype(BF16)
        sc = jnp.concatenate([_mm(qr_bd, buf_ref[slot, pg * ROW_W:pg * ROW_W + KV_W, :].astype(BF16))
                              for pg in range(ppp)], axis=1)
        sc = jnp.where(ok, sc, NEG)
        m_old = m_ref[...]
        m_new = jnp.maximum(m_old, jnp.max(sc, axis=-1, keepdims=True))
        alpha = jnp.exp(m_old - m_new)
        e = jnp.where(ok, jnp.exp(sc - m_new), 0.0)
        m_ref[...] = m_new
        l_ref[...] = alpha * l_ref[...] + jnp.sum(e, axis=-1, keepdims=True)
        eb = e.astype(BF16)
        pv = None
        for pg in range(ppp):
            part_pv = _nt(eb[:, pg * PAGE_SIZE:(pg + 1) * PAGE_SIZE],
                          buf_ref[slot, pg * ROW_W + KV_W:(pg + 1) * ROW_W, :].astype(BF16))
            pv = part_pv if pv is None else pv + part_pv
        acc_ref[...] = alpha * acc_ref[...] + pv

    @pl.when(s == n_steps - 1)
    def _():
        qr_bd = _head_block_diag(qr_ref, ATTN_SCALE).astype(BF16).astype(F32)
        k_new = slcn_ref[0, :, :KV_W].astype(BF16).astype(F32)
        v_new = slcn_ref[0, :, KV_W:].astype(BF16).astype(F32)
        ok_new = sel_ref[:, cur:cur + 1] > 0.5
        s_new = jnp.where(ok_new, jnp.sum(qr_bd * k_new, axis=-1, keepdims=True), NEG)
        m_old = m_ref[...]
        m_new = jnp.maximum(m_old, s_new)
        alpha = jnp.exp(m_old - m_new)
        e_new = jnp.where(ok_new, jnp.exp(s_new - m_new), 0.0)
        l_s = alpha * l_ref[...] + e_new
        o_slc = (alpha * acc_ref[...] + e_new.astype(BF16).astype(F32) * v_new) / l_s
        kpos = past - win_buf + lax.broadcasted_iota(jnp.int32, (1, win_buf), 1)
        w_mask = (kpos <= past) & (kpos > past - WINDOW) & (kpos >= 0)
        sw = jnp.where(w_mask, _nt(qr_bd.astype(BF16), wins_ref[0, :, :KV_W].astype(BF16)), NEG)
        kw_new = winn_ref[0, :, :KV_W].astype(BF16).astype(F32)
        vw_new = winn_ref[0, :, KV_W:].astype(BF16).astype(F32)
        sw_new = jnp.sum(qr_bd * kw_new, axis=-1, keepdims=True)
        mw = jnp.maximum(jnp.max(sw, axis=-1, keepdims=True), sw_new)
        ew = jnp.where(w_mask, jnp.exp(sw - mw), 0.0)
        ew_new = jnp.exp(sw_new - mw)
        lw = jnp.sum(ew, axis=-1, keepdims=True) + ew_new
        o_win = (_mm((ew / lw).astype(BF16), wins_ref[0, :, KV_W:].astype(BF16))
                 + (ew_new / lw).astype(BF16).astype(F32) * vw_new)
        gates = _sigmoid(sm_ref[0])
        hrow = lax.broadcasted_iota(jnp.int32, (N_HEADS, LANES), 0)
        hlane = lax.broadcasted_iota(jnp.int32, (N_HEADS, LANES), 1)
        gcol = [jnp.sum(jnp.where(hlane == SSM_HEADS + br * N_HEADS + hrow, gates, 0.0), axis=-1, keepdims=True)
                for br in range(3)]
        mixed = gcol[0] * ocmp_ref[...] + gcol[1] * o_slc + gcol[2] * o_win
        o_ref[0] = jnp.concatenate(
            [mixed[hd:hd + 1, (hd // GQA) * HEAD_DIM:(hd // GQA + 1) * HEAD_DIM] for hd in range(N_HEADS)], axis=1)


def _nsa_sample_t(qn, qr, u2d, slc_new, win_new, win_state, page_table, cache_cmp_t, cache_slc_t, wbd, pos, kg):
    b = qn.shape[0]
    n_pages = page_table.shape[1]
    past = n_pages * PAGE_SIZE
    ppp = n_pages // N_PARTS
    n_sub = past // CMP_STRIDE
    n_blk = past // SLC_BLOCK + 1
    n_blk_pad = -(-n_blk // LANES) * LANES
    selm, _ = _selection_matrices(n_sub, n_blk_pad, 1)
    selm = selm.at[:, n_sub - 1].set(0.0).T
    hw = N_HEADS * HEAD_DIM
    win_buf = win_state.shape[1]
    n_steps = 2 * N_PARTS
    rp = ppp * PAGE_SIZE
    per_b = lambda w: pl.BlockSpec((1, 1, w), lambda i, s, pt: (i, 0, 0))
    const = lambda shape: pl.BlockSpec(shape, lambda i, s, pt: (0,) * len(shape))
    grid_spec = pltpu.PrefetchScalarGridSpec(
        num_scalar_prefetch=1,
        grid=(b, n_steps),
        in_specs=[per_b(hw), per_b(hw),
                  pl.BlockSpec((1, 1, LANES), lambda i, s, pt: (i, 0, C_SMALL // LANES)),
                  per_b(ROW_W), per_b(ROW_W),
                  pl.BlockSpec((1, win_buf, ROW_W), lambda i, s, pt: (i, 0, 0)),
                  const((CMP_LEN, 2, KV_W, KV_W)), const((CMP_LEN, 2, KV_W)), const((1, KV_W)),
                  const((n_sub, n_blk_pad)),
                  pl.BlockSpec(memory_space=pl.ANY), pl.BlockSpec(memory_space=pl.ANY)],
        out_specs=per_b(hw),
        scratch_shapes=[pltpu.VMEM((2, ppp * ROW_W, LANES), F32), pltpu.SemaphoreType.DMA((2,)),
                        pltpu.VMEM((ROW_TILES, rp, LANES), F32),
                        pltpu.VMEM((4, n_sub, KV_W), F32), pltpu.VMEM((N_HEADS, n_blk_pad), F32),
                        pltpu.VMEM((N_HEADS, KV_W), F32), pltpu.VMEM((N_HEADS, 1), F32),
                        pltpu.VMEM((N_HEADS, 1), F32), pltpu.VMEM((N_HEADS, KV_W), F32)])
    return pl.pallas_call(
        functools.partial(_nsa_sample_t_kernel, past=past, ppp=ppp, n_blk_pad=n_blk_pad, batch=b),
        grid_spec=grid_spec,
        out_shape=jax.ShapeDtypeStruct((b, 1, hw), F32),
        compiler_params=_cparams(("arbitrary", "arbitrary")),
        name="nsa_sample",
    )(page_table, qn.reshape(b, 1, hw), qr.reshape(b, 1, hw), u2d.reshape(b, 1, D_IN_P),
      slc_new.reshape(b, 1, ROW_W), win_new.reshape(b, 1, ROW_W), win_state, wbd, pos, kg, selm,
      cache_cmp_t, cache_slc_t).reshape(b, hw)


def _merge_kernel(x_ref, y_ref, o_ref, mg_ref, wps_ref, wpa_ref, wo_ref, n2_ref, wpq_ref, x2_ref, h2_ref, pq_ref):
    mg = _sigmoid(mg_ref[...])
    mixed = (mg[:, :D_MODEL] * _mm(y_ref[...].astype(BF16), wps_ref[...])
             + mg[:, D_MODEL:] * _mm(o_ref[...].astype(BF16), wpa_ref[...]))
    x2 = x_ref[...] + _mm(mixed.astype(BF16), wo_ref[...])
    x2_ref[...] = x2
    h2 = x2 * lax.rsqrt(jnp.mean(x2 * x2, axis=-1, keepdims=True) + EPS) * n2_ref[...]
    h2_ref[...] = h2
    pq_ref[...] = _mm(h2.astype(BF16), wpq_ref[...])


def _merge(x2d, y_ssm, o_attn, u2d, wps, wpa, wo, norm2_w, wpq):
    n = x2d.shape[0]
    tm = min(256, n)
    rows = lambda w, cb=0: pl.BlockSpec((tm, w), lambda i: (i, cb))
    const = lambda shape: pl.BlockSpec(shape, lambda i: (0, 0))
    out = jax.ShapeDtypeStruct((n, D_MODEL), F32)
    return pl.pallas_call(
        _merge_kernel,
        grid=(n // tm,),
        in_specs=[rows(D_MODEL), rows(D_INNER), rows(N_HEADS * HEAD_DIM), rows(2 * D_MODEL, C_MERGE // (2 * D_MODEL)),
                  const((D_INNER, D_MODEL)), const((N_HEADS * HEAD_DIM, D_MODEL)), const((D_MODEL, D_MODEL)),
                  const((1, D_MODEL)), const((D_MODEL, PEER_HEADS * PEER_DK))],
        out_specs=[rows(D_MODEL), rows(D_MODEL), rows(PEER_HEADS * PEER_DK)],
        out_shape=[out, out, jax.ShapeDtypeStruct((n, PEER_HEADS * PEER_DK), F32)],
        compiler_params=_cparams(("arbitrary",)),
        name="merge",
    )(x2d, y_ssm, o_attn, u2d, wps, wpa, wo, norm2_w.reshape(1, D_MODEL), wpq)


def _top_k_rows(s, k, payload=None):
    r = s.shape[0]
    ridx = lax.broadcasted_iota(jnp.int32, s.shape, 0)
    vals, picks = [], []
    for _ in range(k):
        m = jnp.max(s, axis=0, keepdims=True)
        idx = jnp.min(jnp.where(s == m, ridx, r), axis=0, keepdims=True)
        chosen = ridx == idx
        vals.append(m)
        picks.append(idx if payload is None else jnp.sum(jnp.where(chosen, payload, 0), axis=0, keepdims=True))
        s = jnp.where(chosen, -jnp.inf, s)
    return jnp.concatenate(vals, axis=0), jnp.concatenate(picks, axis=0)


def _peer_select_kernel(pq_ref, sk_ref, e_ref, g_ref):
    half = PEER_DK // 2
    es, gs = [], []
    for hd in range(PEER_HEADS):
        tops = []
        for cidx in range(2):
            col = (hd * 2 + cidx) * half
            q = pq_ref[:, col:col + half].astype(BF16)
            tops.append(_top_k_rows(_nt(sk_ref[hd, cidx], q), PEER_TOPK))
        (s1, i1), (s2, i2) = tops
        n_b = [PEER_TOPK // (a + 1) for a in range(PEER_TOPK)]
        pad = -sum(n_b) % 8
        tn = s1.shape[1]
        cand = jnp.concatenate([s1[a:a + 1] + s2[:n_b[a]] for a in range(PEER_TOPK)]
                               + [jnp.full((pad, tn), -jnp.inf, F32)], axis=0)
        expert = jnp.concatenate([i1[a:a + 1] * N_KEYS + i2[:n_b[a]] for a in range(PEER_TOPK)]
                                 + [jnp.zeros((pad, tn), jnp.int32)], axis=0)
        top_s, top_e = _top_k_rows(cand, PEER_TOPK, payload=expert)
        es.append(top_e)
        ex = jnp.exp(top_s - top_s[0:1])
        gs.append(ex / jnp.sum(ex, axis=0, keepdims=True))
    e_ref[...] = jnp.concatenate(es, axis=0).T
    g_ref[...] = jnp.concatenate(gs, axis=0).T


def _peer_select(pq, sub_keys_bf):
    n = pq.shape[0]
    tn = min(256, n)
    nk = PEER_HEADS * PEER_TOPK
    return pl.pallas_call(
        _peer_select_kernel,
        grid=(n // tn,),
        in_specs=[pl.BlockSpec((tn, PEER_HEADS * PEER_DK), lambda i: (i, 0)),
                  pl.BlockSpec((PEER_HEADS, 2, N_KEYS, PEER_DK // 2), lambda i: (0, 0, 0, 0))],
        out_specs=[pl.BlockSpec((tn, nk), lambda i: (i, 0)), pl.BlockSpec((tn, nk), lambda i: (i, 0))],
        out_shape=[jax.ShapeDtypeStruct((n, nk), jnp.int32), jax.ShapeDtypeStruct((n, nk), F32)],
        compiler_params=_cparams(("arbitrary",)),
        name="peer_select",
    )(pq, sub_keys_bf)


PEER_TOK = 8


FEAT_TILES = D_MODEL // LANES
EXPERT_ROWS = 2 * FEAT_TILES


def _peer_gather_kernel(e_ref, en_ref, g_ref, h_ref, x_ref, uv_hbm, o_ref, buf_ref, sem_ref, *, n):
    i = pl.program_id(0)
    nk = PEER_HEADS * PEER_TOPK
    rows = PEER_TOK * nk * EXPERT_ROWS

    def issue_token(idx_ref, slot, t):
        for k in range(nk):
            src = pl.multiple_of(idx_ref[t, k] * EXPERT_ROWS, EXPERT_ROWS)
            dst = pl.multiple_of((t * nk + k) * EXPERT_ROWS, EXPERT_ROWS)
            pltpu.make_async_copy(uv_hbm.at[pl.ds(src, EXPERT_ROWS)],
                                  buf_ref.at[slot, pl.ds(dst, EXPERT_ROWS)], sem_ref.at[slot]).start(priority=k % 2)

    @pl.when(i == 0)
    def _():
        def tok_body(t, carry):
            issue_token(e_ref, 0, t)
            return carry
        lax.fori_loop(0, PEER_TOK, tok_body, 0)

    @pl.when(i + 1 < n)
    def _():
        for t in range(PEER_TOK):
            issue_token(en_ref, (i + 1) % 2, t)

    slot = i % 2
    pltpu.make_async_copy(uv_hbm.at[pl.ds(0, rows)], buf_ref.at[slot], sem_ref.at[slot]).wait()

    eye = (lax.broadcasted_iota(jnp.int32, (nk, nk), 0) == lax.broadcasted_iota(jnp.int32, (nk, nk), 1)).astype(F32)
    g_t = _nt(eye, g_ref[...], precision=HI)
    for t in range(PEER_TOK):
        tile_row = lambda s, t=t: buf_ref[slot, pl.ds(t * nk * EXPERT_ROWS + s, nk, stride=EXPERT_ROWS), :]
        prod = tile_row(0) * h_ref[t, 0:1, :]
        for s in range(1, FEAT_TILES):
            prod = prod + tile_row(s) * h_ref[t, s:s + 1, :]
        pre = jnp.sum(prod, axis=-1, keepdims=True)
        act = 0.5 * pre * (1.0 + lax.erf(pre * np.float32(math.sqrt(0.5))))
        coef = g_t[:, t:t + 1] * act
        out = jnp.concatenate([jnp.sum(coef * tile_row(FEAT_TILES + s), axis=0, keepdims=True)
                               for s in range(FEAT_TILES)], axis=0)
        o_ref[t] = x_ref[t] + out


def _peer_gather(e, g, h2, x2, uv):
    n = h2.shape[0]
    nk = PEER_HEADS * PEER_TOPK
    steps = n // PEER_TOK
    tiles = lambda: pl.BlockSpec((PEER_TOK, FEAT_TILES, LANES), lambda i: (i, 0, 0))
    smem = lambda imap: pl.BlockSpec((PEER_TOK, nk), imap, memory_space=pltpu.SMEM)
    return pl.pallas_call(
        functools.partial(_peer_gather_kernel, n=steps),
        grid=(steps,),
        in_specs=[smem(lambda i: (i, 0)), smem(lambda i: (jnp.minimum(i + 1, steps - 1), 0)),
                  pl.BlockSpec((PEER_TOK, nk), lambda i: (i, 0)), tiles(), tiles(),
                  pl.BlockSpec(memory_space=pl.ANY)],
        out_specs=tiles(),
        out_shape=jax.ShapeDtypeStruct((n, FEAT_TILES, LANES), F32),
        scratch_shapes=[pltpu.VMEM((2, PEER_TOK * nk * EXPERT_ROWS, LANES), F32), pltpu.SemaphoreType.DMA((2,))],
        compiler_params=_cparams(("arbitrary",)),
        name="peer_gather",
    )(e, e, g, h2.reshape(n, FEAT_TILES, LANES), x2.reshape(n, FEAT_TILES, LANES), uv).reshape(n, D_MODEL)


def _permute_w_in(w_in):
    sizes = (D_INNER, CONV_DIM, SSM_HEADS, N_HEADS * HEAD_DIM, 3 * ROW_W, 3 * N_HEADS, 2 * D_MODEL)
    cuts = np.cumsum((0,) + sizes)
    z, xbc, dt, q, kv, ng, mg = (w_in[:, cuts[i]:cuts[i + 1]] for i in range(7))
    pad = jnp.zeros((D_MODEL, LANES - SSM_HEADS - 3 * N_HEADS), w_in.dtype)
    return jnp.concatenate([xbc, z, mg, q, kv, dt, ng, pad], axis=1).astype(BF16)


def _pack_experts(expert_u, expert_v):
    n = expert_u.shape[0]
    tile = lambda a: a.reshape(n, FEAT_TILES, LANES)
    return jnp.concatenate([tile(expert_u), tile(expert_v)], axis=1).reshape(n * EXPERT_ROWS, LANES)


def _layer_weights(norm1_w, w_in, conv_w, conv_b, dt_bias, a_log, d_skip, ssm_norm_w, w_proj_ssm, q_norm_w,
                   k_norm_w, cmp_pos, w_cmp, w_proj_attn, w_out, norm2_w, w_pq, sub_keys, expert_u, expert_v):
    wbd, pos, kg = _compress_weights(w_cmp, cmp_pos, k_norm_w[0])
    return dict(norm1_w=norm1_w, w_in=_permute_w_in(w_in), conv_w=conv_w, conv_b=conv_b, dt_bias=dt_bias,
                a_log=a_log, d_skip=d_skip, ssm_norm_w=ssm_norm_w, wps=w_proj_ssm.astype(BF16),
                q_norm_w=q_norm_w, k_norm_w=k_norm_w, wbd=wbd, pos=pos, kg=kg, wpa=w_proj_attn.astype(BF16),
                wo=w_out.astype(BF16), norm2_w=norm2_w, wpq=w_pq.astype(BF16), sub_keys=sub_keys.astype(BF16),
                uv=_pack_experts(expert_u, expert_v))


def _token_tail(x2d, y_ssm, o_attn, u2d, w):
    x2, h2, pq = _merge(x2d, y_ssm, o_attn, u2d, w["wps"], w["wpa"], w["wo"], w["norm2_w"], w["wpq"])
    e, g = _peer_select(pq, w["sub_keys"])
    return _peer_gather(e, g, h2, x2, w["uv"])


def _layer_prompt(x, w, win_buf):
    b, t = x.shape[:2]
    assert t % (2 * Q_BLOCK) == 0 and t >= WINDOW + Q_BLOCK and t >= win_buf
    x2d = x.reshape(b * t, D_MODEL)
    u2d = _inproj(x2d, w["norm1_w"], w["w_in"])
    u3 = u2d.reshape(b, t, D_IN_P)
    y_ssm, h_last = _ssd(u3, jnp.zeros((b, D_INNER, D_STATE), F32), jnp.zeros((b, 8, CONV_DIM), F32),
                         w["conv_w"], w["conv_b"], w["dt_bias"], w["a_log"], w["d_skip"], w["ssm_norm_w"])
    cos_t, sin_t = _rope_tables(jnp.arange(t, dtype=jnp.int32))
    qn, qr, slc, win = _nsa_prep(u2d, cos_t, sin_t, w["q_norm_w"], w["k_norm_w"], t)
    cmp_rows = u3[:, :, C_KV:C_KV + ROW_W]
    kc, vc = _compress_prompt(cmp_rows, w["wbd"], w["pos"], w["kg"])
    hw = N_HEADS * HEAD_DIM
    o_attn = _nsa_prompt(qn.reshape(b, t, hw), qr.reshape(b, t, hw), u3, kc, vc,
                         slc.reshape(b, t, ROW_W), win.reshape(b, t, ROW_W))
    y = _token_tail(x2d, y_ssm.reshape(b * t, D_INNER), o_attn.reshape(b * t, hw), u2d, w)
    rows = lambda a: a.reshape(b, -1, 2, N_KV, HEAD_DIM)
    state = (rows(cmp_rows), rows(slc), rows(win.reshape(b, t, ROW_W)[:, t - win_buf:]),
             h_last.reshape(b, SSM_HEADS, SSM_HEADDIM, D_STATE), u3[:, t - (CONV_W - 1):, C_XBC:C_XBC + CONV_DIM])
    return y.reshape(b, t, D_MODEL), state


def _layer_sample(x, w, cache_cmp, cache_slc, page_table, win_state, h0, conv_state):
    b, t = x.shape[:2]
    assert t == 1
    past = page_table.shape[1] * PAGE_SIZE
    win_buf = win_state.shape[1]
    assert page_table.shape[1] % N_PARTS == 0 and win_buf >= 1
    x2d = x.reshape(b, D_MODEL)
    u2d = _inproj(x2d, w["norm1_w"], w["w_in"])
    u3 = u2d.reshape(b, 1, D_IN_P)
    prefix8 = jnp.pad(conv_state, ((0, 0), (8 - (CONV_W - 1), 0), (0, 0)))
    y_ssm, h_last = _ssd(u3, h0.reshape(b, D_INNER, D_STATE), prefix8, w["conv_w"], w["conv_b"], w["dt_bias"],
                         w["a_log"], w["d_skip"], w["ssm_norm_w"])
    cos_t, sin_t = _rope_tables(jnp.full((1,), past, jnp.int32))
    qn, qr, slc, win = _nsa_prep(u2d, cos_t, sin_t, w["q_norm_w"], w["k_norm_w"], 1)
    n_pool = cache_cmp.shape[0]
    page_t = lambda cache: jnp.transpose(cache, (0, 2, 3, 4, 1)).reshape(n_pool, ROW_W, PAGE_SIZE)
    o_attn = _nsa_sample_t(qn, qr, u2d, slc, win, win_state.reshape(b, win_buf, ROW_W), page_table,
                           page_t(cache_cmp), page_t(cache_slc), w["wbd"], w["pos"], w["kg"])
    y = _token_tail(x2d, y_ssm.reshape(b, D_INNER), o_attn, u2d, w)
    rows = lambda a: a.reshape(b, -1, 2, N_KV, HEAD_DIM)
    new_win = jnp.concatenate([win_state.reshape(b, win_buf, ROW_W), win.reshape(b, 1, ROW_W)], axis=1)[:, 1:]
    new_conv = jnp.concatenate([conv_state, u3[:, :, C_XBC:C_XBC + CONV_DIM]], axis=1)[:, 1:]
    state = (rows(u2d[:, C_KV:C_KV + ROW_W]), rows(slc), rows(new_win),
             h_last.reshape(b, SSM_HEADS, SSM_HEADDIM, D_STATE), new_conv)
    return y.reshape(b, 1, D_MODEL), state


def kernel(x_prompt, x_sample, cache_cmp_kv, cache_slc_kv, page_table, state_win_kv, state_ssm, state_conv,
           norm1_w, w_in, conv_w, conv_b, dt_bias, a_log, d_skip, ssm_norm_w, w_proj_ssm, q_norm_w, k_norm_w,
           cmp_pos, w_cmp, w_proj_attn, w_out, norm2_w, w_pq, sub_keys, expert_u, expert_v):
    depth = w_in.shape[0]
    win_buf = state_win_kv.shape[2]
    yp, ys = x_prompt, x_sample
    new_p, new_s = [], []
    for l in range(depth):
        w = _layer_weights(norm1_w[l], w_in[l], conv_w[l], conv_b[l], dt_bias[l], a_log[l], d_skip[l],
                           ssm_norm_w[l], w_proj_ssm[l], q_norm_w[l], k_norm_w[l], cmp_pos[l], w_cmp[l],
                           w_proj_attn[l], w_out[l], norm2_w[l], w_pq[l], sub_keys[l], expert_u[l], expert_v[l])
        yp, st_p = _layer_prompt(yp, w, win_buf)
        ys, st_s = _layer_sample(ys, w, cache_cmp_kv[l], cache_slc_kv[l], page_table, state_win_kv[l],
                                 state_ssm[l], state_conv[l])
        new_p.append(st_p)
        new_s.append(st_s)
    stack = lambda states, i: jnp.stack([s[i] for s in states], axis=0)
    return (yp, ys, stack(new_p, 0), stack(new_s, 0), stack(new_p, 1), stack(new_s, 1), stack(new_p, 2),
            stack(new_s, 2), stack(new_p, 3), stack(new_s, 3), stack(new_p, 4), stack(new_s, 4))
```
